```python
import jax
import jax.numpy as jnp
from jax import lax
import numpy as np

D_MODEL = 4096
BATCH = 4
SEQ = 4096
DEPTH = 4

CTX_LEN = 256
GRID_W = 64
NA_HEAD_DIM = 128
NA_HEADS = D_MODEL // NA_HEAD_DIM
WIN_R = 8
WIN_C = 16
RET_HEADS = 16
RET_HEAD_DIM = D_MODEL // RET_HEADS
RET_CHUNK = 128
FFN_DIM = D_MODEL
N_EXPERTS = 8
TOP_K = 2
EXPERT_DIM = D_MODEL // 8
ROPE_BASE = 10000.0
EPS = 1e-6
NEG_INF = -1e30
N_NA = (DEPTH + 1) // 2
N_RET = DEPTH // 2

kernel_name = "hybrid_natten_retention_moe_dit"


def rms_norm(x, w):
    xf = x.astype(jnp.float32)
    y = xf * lax.rsqrt(jnp.mean(xf * xf, axis=-1, keepdims=True) + EPS)
    return y.astype(x.dtype) * w


def modulate(x, shift, scale):
    return x * (1.0 + scale) + shift


def axial_rope_tables(n_tokens, head_dim):
    t = jnp.arange(n_tokens)
    row = (t // GRID_W).astype(jnp.float32)
    col = (t % GRID_W).astype(jnp.float32)
    axis_dim = head_dim // 2
    inv_freq = jnp.power(ROPE_BASE, -jnp.arange(0, axis_dim, 2, dtype=jnp.float32) / axis_dim)
    ang_r = row[:, None] * inv_freq[None, :]
    ang_c = col[:, None] * inv_freq[None, :]
    return (jnp.cos(ang_r), jnp.sin(ang_r), jnp.cos(ang_c), jnp.sin(ang_c))


def _rotate(x, cos, sin):
    x1, x2 = jnp.split(x, 2, axis=-1)
    return jnp.concatenate([x1 * cos - x2 * sin, x1 * sin + x2 * cos], axis=-1)


def apply_axial_rope(x, rope):
    cos_r, sin_r, cos_c, sin_c = rope
    xr, xc = jnp.split(x, 2, axis=-1)
    out = jnp.concatenate([_rotate(xr, cos_r, sin_r), _rotate(xc, cos_c, sin_c)], axis=-1)
    return out.astype(x.dtype)


def na_latent_attention(q, k, v, kc, vc, rpb):
    B, H, S, hd = q.shape
    rows = S // GRID_W
    wr = min(WIN_R, rows)
    scale = hd ** -0.5
    qb = q.reshape(B, H, rows, GRID_W, hd)
    kb = k.reshape(B, H, rows, GRID_W, hd)
    vb = v.reshape(B, H, rows, GRID_W, hd)
    cols = jnp.arange(GRID_W)
    c0 = jnp.clip(cols - WIN_C // 2, 0, GRID_W - WIN_C)
    col_in = (cols[None, :] >= c0[:, None]) & (cols[None, :] < c0[:, None] + WIN_C)
    dc = jnp.clip(cols[None, :] - cols[:, None] + WIN_C - 1, 0, 2 * WIN_C - 2)
    col_bias = rpb[:, :, dc].astype(jnp.float32)

    def row_block(r):
        r0 = jnp.clip(r - WIN_R // 2, 0, rows - wr)
        q_r = lax.dynamic_index_in_dim(qb, r, axis=2, keepdims=False)
        k_r = lax.dynamic_slice_in_dim(kb, r0, wr, axis=2)
        v_r = lax.dynamic_slice_in_dim(vb, r0, wr, axis=2)
        dr = r0 + jnp.arange(wr) - r + (WIN_R - 1)
        bias = jnp.take(col_bias, dr, axis=1).transpose(0, 2, 1, 3)
        s_win = jnp.einsum("bhqd,bhrkd->bhqrk", q_r, k_r).astype(jnp.float32) * scale + bias[None]
        s_win = jnp.where(col_in[:, None, :], s_win, NEG_INF)
        s_ctx = jnp.einsum("bhqd,bhld->bhql", q_r, kc).astype(jnp.float32) * scale
        s = jnp.concatenate([s_win.reshape(B, H, GRID_W, wr * GRID_W), s_ctx], axis=-1)
        p = jax.nn.softmax(s, axis=-1).astype(v.dtype)
        p_win = p[..., : wr * GRID_W].reshape(B, H, GRID_W, wr, GRID_W)
        p_ctx = p[..., wr * GRID_W:]
        return (jnp.einsum("bhqrk,bhrkd->bhqd", p_win, v_r)
                + jnp.einsum("bhql,bhld->bhqd", p_ctx, vc))

    out = lax.map(row_block, jnp.arange(rows))
    return out.transpose(1, 2, 0, 3, 4).reshape(B, H, S, hd)


def neighbourhood_attention_mixer(u, uc, w_qkv, w_o, qn_w, kn_w, rpb, ctx_out):
    B, S, D = u.shape
    L = uc.shape[1]

    def project(t):
        n = t.shape[1]
        q, k, v = jnp.split(t @ w_qkv, 3, axis=-1)
        q = rms_norm(q.reshape(B, n, NA_HEADS, NA_HEAD_DIM), qn_w)
        k = rms_norm(k.reshape(B, n, NA_HEADS, NA_HEAD_DIM), kn_w)
        v = v.reshape(B, n, NA_HEADS, NA_HEAD_DIM)
        return [a.transpose(0, 2, 1, 3) for a in (q, k, v)]

    q, k, v = project(u)
    qc, kc, vc = project(uc)
    o = na_latent_attention(q, k, v, kc, vc, rpb)
    o = o.transpose(0, 2, 1, 3).reshape(B, S, D) @ w_o
    oc = None
    if ctx_out:
        s = jnp.einsum("bhqd,bhkd->bhqk", qc, kc).astype(jnp.float32) * NA_HEAD_DIM ** -0.5
        p = jax.nn.softmax(s, axis=-1).astype(vc.dtype)
        oc = jnp.einsum("bhqk,bhkd->bhqd", p, vc).transpose(0, 2, 1, 3).reshape(B, L, D) @ w_o
    return o, oc


def retention_scan(q, k, v, log_gamma, state0, strict):
    B, H, N, dk = q.shape
    dv = v.shape[-1]
    C = min(RET_CHUNK, N)
    nc = N // C
    i = jnp.arange(C, dtype=jnp.float32)
    diff = i[:, None] - i[None, :]
    keep = (diff > 0) if strict else (diff >= 0)
    intra = jnp.where(keep, jnp.exp(log_gamma[:, None, None] * jnp.maximum(diff, 0.0)), 0.0)
    q_dec = jnp.exp(log_gamma[:, None] * (i + 1.0))[:, :, None]
    k_dec = jnp.exp(log_gamma[:, None] * (C - 1.0 - i))[:, :, None]
    chunk_dec = jnp.exp(log_gamma * C)[:, None, None]

    def chunks(a):
        return a.reshape(B, H, nc, C, a.shape[-1]).transpose(2, 0, 1, 3, 4)

    def step(state, inp):
        qi, ki, vi = inp
        s = jnp.einsum("bhid,bhjd->bhij", qi, ki) * intra
        o = jnp.einsum("bhij,bhjv->bhiv", s, vi) + jnp.einsum("bhid,bhdv->bhiv", qi * q_dec, state)
        state = state * chunk_dec + jnp.einsum("bhjd,bhjv->bhdv", ki * k_dec, vi)
        return state, o

    _, o = lax.scan(step, state0, (chunks(q), chunks(k), chunks(v)))
    return o.transpose(1, 2, 0, 3, 4).reshape(B, H, N, dv)


def bidirectional_retention(q, k, v, lg_f, lg_b, st_f, st_b):
    fwd = retention_scan(q, k, v, lg_f, st_f, False)
    bwd = retention_scan(jnp.flip(q, axis=2), jnp.flip(k, axis=2), jnp.flip(v, axis=2), lg_b, st_b, True)
    return fwd + jnp.flip(bwd, axis=2)


def retention_mixer(u, uc, w_qkvg, w_o, logit_f, logit_b, gn_w, rope, ctx_out):
    B, S, D = u.shape
    L = uc.shape[1]

    def project(t):
        n = t.shape[1]
        q, k, v, g = jnp.split(t @ w_qkvg, 4, axis=-1)

        def heads(a):
            return a.reshape(B, n, RET_HEADS, RET_HEAD_DIM).transpose(0, 2, 1, 3)
        return heads(q), heads(k) * RET_HEAD_DIM ** -0.5, heads(v), g

    def out_proj(o, g):
        n = o.shape[2]
        mu = jnp.mean(o, axis=-1, keepdims=True)
        var = jnp.mean(jnp.square(o - mu), axis=-1, keepdims=True)
        o = ((o - mu) * lax.rsqrt(var + EPS)).transpose(0, 2, 1, 3).reshape(B, n, D)
        return (o.astype(g.dtype) * gn_w * jax.nn.silu(g)) @ w_o

    q, k, v, g = project(u)
    qc, kc, vc, gc = project(uc)
    q = apply_axial_rope(q, rope)
    k = apply_axial_rope(k, rope)
    lg_f = jax.nn.log_sigmoid(logit_f.astype(jnp.float32))
    lg_b = jax.nn.log_sigmoid(logit_b.astype(jnp.float32))
    pos = jnp.arange(L, dtype=jnp.float32)
    w_f = jnp.exp(lg_f[:, None] * (L - 1.0 - pos))[None, :, :, None]
    w_b = jnp.exp(lg_b[:, None] * pos)[None, :, :, None]
    st_f = jnp.einsum("bhld,bhlv->bhdv", kc * w_f, vc)
    st_b = jnp.einsum("bhld,bhlv->bhdv", kc * w_b, vc)
    o = out_proj(bidirectional_retention(q, k, v, lg_f, lg_b, st_f, st_b), g)
    oc = None
    if ctx_out:
        zero = jnp.zeros_like(st_f)
        oc = out_proj(bidirectional_retention(qc, kc, vc, lg_f, lg_b, zero, zero), gc)
    return o, oc


def swiglu(u, w13, w2):
    a, b = jnp.split(u @ w13, 2, axis=-1)
    return (jax.nn.silu(a) * b) @ w2


def moe_swiglu(u, w_router, w13, w2):
    shape = u.shape
    t = u.reshape(-1, shape[-1])
    probs = jax.nn.softmax((t @ w_router).astype(jnp.float32), axis=-1)
    top_p, top_i = lax.top_k(probs, TOP_K)
    top_p = top_p / jnp.sum(top_p, axis=-1, keepdims=True)
    combine = jnp.sum(jax.nn.one_hot(top_i, N_EXPERTS, dtype=jnp.float32) * top_p[..., None], axis=1)
    a, b = jnp.split(jnp.einsum("td,edf->tef", t, w13), 2, axis=-1)
    act = jax.nn.silu(a) * b * combine[..., None].astype(t.dtype)
    return jnp.einsum("tef,efd->td", act, w2).reshape(shape)


def setup_inputs(seed: int = 0) -> dict:
    key = jax.random.key(seed)
    ks = jax.random.split(key, 24)
    D = D_MODEL
    f32 = jnp.float32

    def nrm(k, shape, s):
        return jax.random.normal(k, shape, f32) * s

    def gain(k, shape):
        return 1.0 + nrm(k, shape, 0.02)

    base_logit = jnp.log(jnp.power(2.0, 5.0 + jnp.arange(RET_HEADS, dtype=f32)) - 1.0)
    return {
        "x": nrm(ks[0], (BATCH, SEQ, D), 1.0),
        "c": nrm(ks[1], (BATCH, D), 1.0),
        "ctx": nrm(ks[2], (BATCH, CTX_LEN, D), 1.0),
        "c_ctx": nrm(ks[3], (D,), 1.0),
        "ada_w": nrm(ks[4], (DEPTH, D, 6 * D), 0.5 * D ** -0.5),
        "ada_b": nrm(ks[5], (DEPTH, 6 * D), 0.02),
        "norm1_w": gain(ks[6], (DEPTH, D)),
        "norm2_w": gain(ks[7], (DEPTH, D)),
        "na_wqkv": nrm(ks[8], (N_NA, D, 3 * D), D ** -0.5),
        "na_wo": nrm(ks[9], (N_NA, D, D), D ** -0.5),
        "na_qnorm_w": gain(ks[10], (N_NA, NA_HEAD_DIM)),
        "na_knorm_w": gain(ks[11], (N_NA, NA_HEAD_DIM)),
        "na_rpb": nrm(ks[12], (N_NA, NA_HEADS, 2 * WIN_R - 1, 2 * WIN_C - 1), 0.1),
        "ret_wqkvg": nrm(ks[13], (N_RET, D, 4 * D), D ** -0.5),
        "ret_wo": nrm(ks[14], (N_RET, D, D), D ** -0.5),
        "ret_decay_f": base_logit + nrm(ks[15], (N_RET, RET_HEADS), 0.1),
        "ret_decay_b": base_logit + nrm(ks[16], (N_RET, RET_HEADS), 0.1),
        "ret_gn_w": gain(ks[17], (N_RET, D)),
        "ffn_w13": nrm(ks[18], (N_NA, D, 2 * FFN_DIM), D ** -0.5),
        "ffn_w2": nrm(ks[19], (N_NA, FFN_DIM, D), FFN_DIM ** -0.5),
        "moe_router": nrm(ks[20], (N_RET, D, N_EXPERTS), D ** -0.5),
        "moe_w13": nrm(ks[21], (N_RET, N_EXPERTS, D, 2 * EXPERT_DIM), D ** -0.5),
        "moe_w2": nrm(ks[22], (N_RET, N_EXPERTS, EXPERT_DIM, D), EXPERT_DIM ** -0.5),
    }


def reference(x, c, ctx, c_ctx, ada_w, ada_b, norm1_w, norm2_w, na_wqkv, na_wo, na_qnorm_w,
              na_knorm_w, na_rpb, ret_wqkvg, ret_wo, ret_decay_f, ret_decay_b, ret_gn_w,
              ffn_w13, ffn_w2, moe_router, moe_w13, moe_w2):
    S = x.shape[1]
    rope = axial_rope_tables(S, RET_HEAD_DIM)
    c_act = jax.nn.silu(c)
    cc_act = jax.nn.silu(c_ctx)
    h, hc = x, ctx
    for i in range(DEPTH):
        j = i // 2
        ctx_out = i < DEPTH - 1
        sh1, sc1, g1, sh2, sc2, g2 = jnp.split(c_act @ ada_w[i] + ada_b[i], 6, axis=-1)
        csh1, csc1, cg1, csh2, csc2, cg2 = jnp.split(cc_act @ ada_w[i] + ada_b[i], 6, axis=-1)

        u = modulate(rms_norm(h, norm1_w[i]), sh1[:, None, :], sc1[:, None, :])
        uc = modulate(rms_norm(hc, norm1_w[i]), csh1, csc1)
        if i % 2 == 0:
            o, oc = neighbourhood_attention_mixer(u, uc, na_wqkv[j], na_wo[j], na_qnorm_w[j],
                                                  na_knorm_w[j], na_rpb[j], ctx_out)
        else:
            o, oc = retention_mixer(u, uc, ret_wqkvg[j], ret_wo[j], ret_decay_f[j], ret_decay_b[j],
                                    ret_gn_w[j], rope, ctx_out)
        h = h + g1[:, None, :] * o
        if ctx_out:
            hc = hc + cg1 * oc

        u2 = modulate(rms_norm(h, norm2_w[i]), sh2[:, None, :], sc2[:, None, :])
        if i % 2 == 0:
            h = h + g2[:, None, :] * swiglu(u2, ffn_w13[j], ffn_w2[j])
        else:
            h = h + g2[:, None, :] * moe_swiglu(u2, moe_router[j], moe_w13[j], moe_w2[j])
        if ctx_out:
            uc2 = modulate(rms_norm(hc, norm2_w[i]), csh2, csc2)
            if i % 2 == 0:
                hc = hc + cg2 * swiglu(uc2, ffn_w13[j], ffn_w2[j])
            else:
                hc = hc + cg2 * moe_swiglu(uc2, moe_router[j], moe_w13[j], moe_w2[j])
    return h
```

```python
import functools

import jax
import jax.numpy as jnp
from jax import lax
from jax.experimental import pallas as pl
from jax.experimental.pallas import tpu as pltpu

F32 = jnp.float32
BF16 = jnp.bfloat16

GRID_W = 64
RET_CHUNK = 128
ROPE_BASE = 10000.0
EPS = 1e-6
NEG_INF = -1e30

V7X_VMEM_BYTES = 64 * 1024 * 1024
VMEM_LIMIT_BYTES = V7X_VMEM_BYTES * 7 // 8
LANES = 128
ADA_ROWS = 8


def _params(n_grid_dims):
    return pltpu.CompilerParams(
        dimension_semantics=("arbitrary",) * n_grid_dims,
        vmem_limit_bytes=VMEM_LIMIT_BYTES,
    )


def _tile(n, pref):
    t = min(pref, n)
    while n % t:
        t //= 2
    return t


def _silu(x):
    return x * (1.0 / (1.0 + jnp.exp(-x)))


def _ada_kernel(x_ref, w_ref, b_ref, o_ref):
    xa = _silu(x_ref[...]).astype(BF16)
    w = w_ref[...].astype(BF16)
    o_ref[...] = jnp.dot(xa, w, preferred_element_type=F32) + b_ref[...]


def _ada_params(cond, ada_w, ada_b):
    depth, d, n = ada_w.shape
    bn = _tile(n, 512)
    return pl.pallas_call(
        _ada_kernel,
        out_shape=jax.ShapeDtypeStruct((depth, ADA_ROWS, n), F32),
        grid=(depth, n // bn),
        in_specs=[
            pl.BlockSpec((ADA_ROWS, d), lambda l, j: (0, 0)),
            pl.BlockSpec((None, d, bn), lambda l, j: (l, 0, j)),
            pl.BlockSpec((None, 1, bn), lambda l, j: (l, 0, j)),
        ],
        out_specs=pl.BlockSpec((None, ADA_ROWS, bn), lambda l, j: (l, 0, j)),
        compiler_params=_params(2),
        name="ada_params",
    )(cond, ada_w, ada_b.reshape(depth, 1, n))


def _norm_mod_kernel(x_ref, nw_ref, sh_ref, sc_ref, *rest, n_experts):
    x = x_ref[...]
    y = x * lax.rsqrt(jnp.mean(x * x, axis=-1, keepdims=True) + EPS)
    u = (y * nw_ref[...]) * (1.0 + sc_ref[...]) + sh_ref[...]
    ub = u.astype(BF16)
    if n_experts == 0:
        (u_ref,) = rest
        u_ref[...] = ub
        return
    wr_ref, u_ref, comb_ref = rest
    u_ref[...] = ub
    logits = jnp.dot(ub, wr_ref[...], preferred_element_type=F32)
    lane = lax.broadcasted_iota(jnp.int32, logits.shape, 1).astype(F32)
    valid = lane < n_experts
    logits = jnp.where(valid, logits, NEG_INF)
    e = jnp.exp(logits - jnp.max(logits, axis=-1, keepdims=True))
    probs = jnp.where(valid, e / jnp.sum(e, axis=-1, keepdims=True), -1.0)
    p1 = jnp.max(probs, axis=-1, keepdims=True)
    i1 = jnp.min(jnp.where(probs == p1, lane, float(LANES)), axis=-1, keepdims=True)
    rest_p = jnp.where(lane == i1, -1.0, probs)
    p2 = jnp.max(rest_p, axis=-1, keepdims=True)
    i2 = jnp.min(jnp.where(rest_p == p2, lane, float(LANES)), axis=-1, keepdims=True)
    inv = 1.0 / (p1 + p2)
    comb_ref[...] = jnp.where(lane == i1, p1 * inv, jnp.where(lane == i2, p2 * inv, 0.0))


def _norm_mod(x, norm_w, shift, scale, rows_per_batch, router_w=None):
    m, d = x.shape
    bm = _tile(rows_per_batch, 256)
    tiles_per_batch = rows_per_batch // bm
    mod_spec = pl.BlockSpec((None, 1, d), lambda i: (i // tiles_per_batch, 0, 0))
    in_specs = [
        pl.BlockSpec((bm, d), lambda i: (i, 0)),
        pl.BlockSpec((1, d), lambda i: (0, 0)),
        mod_spec,
        mod_spec,
    ]
    args = [x, norm_w.reshape(1, d), shift, scale]
    u_shape = jax.ShapeDtypeStruct((m, d), BF16)
    u_spec = pl.BlockSpec((bm, d), lambda i: (i, 0))
    if router_w is None:
        n_experts = 0
        out_shape, out_specs = u_shape, u_spec
    else:
        n_experts = router_w.shape[1]
        wr = jnp.zeros((d, LANES), BF16).at[:, :n_experts].set(router_w.astype(BF16))
        in_specs.append(pl.BlockSpec((d, LANES), lambda i: (0, 0)))
        args.append(wr)
        out_shape = (u_shape, jax.ShapeDtypeStruct((m, LANES), F32))
        out_specs = (u_spec, pl.BlockSpec((bm, LANES), lambda i: (i, 0)))
    return pl.pallas_call(
        functools.partial(_norm_mod_kernel, n_experts=n_experts),
        out_shape=out_shape,
        grid=(m // bm,),
        in_specs=in_specs,
        out_specs=out_specs,
        compiler_params=_params(1),
        name="norm_mod",
    )(*args)


def _mm_na_qkv_kernel(x_ref, w_ref, qn_ref, kn_ref, o_ref, *, tiles_per_part, hd):
    j = pl.program_id(1)
    acc = jnp.dot(x_ref[...], w_ref[...], preferred_element_type=F32)
    n_heads = acc.shape[1] // hd

    def store_normed(nw):
        for t in range(n_heads):
            a = acc[:, t * hd:(t + 1) * hd]
            y = a * lax.rsqrt(jnp.mean(a * a, axis=-1, keepdims=True) + EPS)
            o_ref[:, t * hd:(t + 1) * hd] = (y * nw).astype(BF16)

    @pl.when(j < tiles_per_part)
    def _():
        store_normed(qn_ref[...])

    @pl.when((j >= tiles_per_part) & (j < 2 * tiles_per_part))
    def _():
        store_normed(kn_ref[...])

    @pl.when(j >= 2 * tiles_per_part)
    def _():
        o_ref[...] = acc.astype(BF16)


def _mm_na_qkv(x, w, qn_w, kn_w):
    m, k = x.shape
    n = w.shape[1]
    hd = qn_w.shape[0]
    bm, bn = _tile(m, 1024), _tile(n // 3, 1024)
    return pl.pallas_call(
        functools.partial(_mm_na_qkv_kernel, tiles_per_part=n // 3 // bn, hd=hd),
        out_shape=jax.ShapeDtypeStruct((m, n), BF16),
        grid=(m // bm, n // bn),
        in_specs=[
            pl.BlockSpec((bm, k), lambda i, j: (i, 0)),
            pl.BlockSpec((k, bn), lambda i, j: (0, j)),
            pl.BlockSpec((1, hd), lambda i, j: (0, 0)),
            pl.BlockSpec((1, hd), lambda i, j: (0, 0)),
        ],
        out_specs=pl.BlockSpec((bm, bn), lambda i, j: (i, j)),
        compiler_params=_params(2),
        name="mm_na_qkv",
    )(x, w, qn_w.reshape(1, hd), kn_w.reshape(1, hd))


def _mm_ret_qkvg_kernel(x_ref, w_ref, *rest, tiles_per_part, hd, rope):
    if rope:
        cos_ref, sin_ref, o_ref = rest
    else:
        (o_ref,) = rest
    j = pl.program_id(1)
    acc = jnp.dot(x_ref[...], w_ref[...], preferred_element_type=F32)
    n_heads = acc.shape[1] // hd
    k_scale = hd ** -0.5

    def store_qk(scale):
        for t in range(n_heads):
            a = acc[:, t * hd:(t + 1) * hd]
            if rope:
                lane = lax.broadcasted_iota(jnp.int32, a.shape, 1)
                first = (lane % (hd // 2)) < (hd // 4)
                partner = jnp.where(first, pltpu.roll(a, hd - hd // 4, 1), pltpu.roll(a, hd // 4, 1))
                a = a * cos_ref[...] + partner * sin_ref[...]
            o_ref[:, t * hd:(t + 1) * hd] = (a * scale).astype(BF16)

    @pl.when(j < tiles_per_part)
    def _():
        store_qk(1.0)

    @pl.when((j >= tiles_per_part) & (j < 2 * tiles_per_part))
    def _():
        store_qk(k_scale)

    @pl.when((j >= 2 * tiles_per_part) & (j < 3 * tiles_per_part))
    def _():
        o_ref[...] = acc.astype(BF16)

    @pl.when(j >= 3 * tiles_per_part)
    def _():
        o_ref[...] = _silu(acc).astype(BF16)


def _mm_ret_qkvg(x, w, hd, rope_tables, seq):
    m, k = x.shape
    n = w.shape[1]
    bm, bn = _tile(seq, 1024), _tile(n // 4, 1024)
    rope = rope_tables is not None
    in_specs = [
        pl.BlockSpec((bm, k), lambda i, j: (i, 0)),
        pl.BlockSpec((k, bn), lambda i, j: (0, j)),
    ]
    args = [x, w]
    if rope:
        tiles_per_seq = seq // bm
        tab_spec = pl.BlockSpec((bm, hd), lambda i, j: (i % tiles_per_seq, 0))
        in_specs += [tab_spec, tab_spec]
        args += list(rope_tables)
    return pl.pallas_call(
        functools.partial(_mm_ret_qkvg_kernel, tiles_per_part=n // 4 // bn, hd=hd, rope=rope),
        out_shape=jax.ShapeDtypeStruct((m, n), BF16),
        grid=(m // bm, n // bn),
        in_specs=in_specs,
        out_specs=pl.BlockSpec((bm, bn), lambda i, j: (i, j)),
        compiler_params=_params(2),
        name="mm_ret_qkvg",
    )(*args)


def _mm_swiglu_kernel(x_ref, wa_ref, wb_ref, *rest, tiles_per_expert):
    x = x_ref[...]
    a = jnp.dot(x, wa_ref[...], preferred_element_type=F32)
    b = jnp.dot(x, wb_ref[...], preferred_element_type=F32)
    act = _silu(a) * b
    if tiles_per_expert:
        comb_ref, o_ref = rest
        e = pl.program_id(1) // tiles_per_expert
        comb = comb_ref[...]
        lane = lax.broadcasted_iota(jnp.int32, comb.shape, 1)
        act = act * jnp.sum(jnp.where(lane == e, comb, 0.0), axis=-1, keepdims=True)
    else:
        (o_ref,) = rest
    o_ref[...] = act.astype(BF16)


def _mm_swiglu(x, w13, comb=None):
    m, k = x.shape
    bm = _tile(m, 1024)
    if comb is None:
        f = w13.shape[1] // 2
        bn = _tile(f, 512)
        nt = f // bn
        wa_spec = pl.BlockSpec((k, bn), lambda i, j: (0, j))
        wb_spec = pl.BlockSpec((k, bn), lambda i, j: (0, j + nt))
        in_specs = [pl.BlockSpec((bm, k), lambda i, j: (i, 0)), wa_spec, wb_spec]
        args = [x, w13, w13]
        tpe = 0
        n_out = f
    else:
        n_e, _, ed2 = w13.shape
        ed = ed2 // 2
        bn = _tile(ed, 512)
        tpe = ed // bn
        wa_spec = pl.BlockSpec((None, k, bn), lambda i, j: (j // tpe, 0, j % tpe))
        wb_spec = pl.BlockSpec((None, k, bn), lambda i, j: (j // tpe, 0, tpe + j % tpe))
        in_specs = [pl.BlockSpec((bm, k), lambda i, j: (i, 0)), wa_spec, wb_spec,
                    pl.BlockSpec((bm, LANES), lambda i, j: (i, 0))]
        args = [x, w13, w13, comb]
        n_out = n_e * ed
    return pl.pallas_call(
        functools.partial(_mm_swiglu_kernel, tiles_per_expert=tpe),
        out_shape=jax.ShapeDtypeStruct((m, n_out), BF16),
        grid=(m // bm, n_out // bn),
        in_specs=in_specs,
        out_specs=pl.BlockSpec((bm, bn), lambda i, j: (i, j)),
        compiler_params=_params(2),
        name="mm_swiglu",
    )(*args)


def _mm_res_kernel(x_ref, w_ref, res_ref, gate_ref, o_ref):
    acc = jnp.dot(x_ref[...], w_ref[...], preferred_element_type=F32)
    o_ref[...] = res_ref[...] + gate_ref[...] * acc


def _mm_res(x, w, res, gate, rows_per_batch):
    m, k = x.shape
    n = w.shape[1]
    bm, bn = _tile(rows_per_batch, 1024), _tile(n, 512)
    tiles_per_batch = rows_per_batch // bm
    return pl.pallas_call(
        _mm_res_kernel,
        out_shape=jax.ShapeDtypeStruct((m, n), F32),
        grid=(m // bm, n // bn),
        in_specs=[
            pl.BlockSpec((bm, k), lambda i, j: (i, 0)),
            pl.BlockSpec((k, bn), lambda i, j: (0, j)),
            pl.BlockSpec((bm, bn), lambda i, j: (i, j)),
            pl.BlockSpec((None, 1, bn), lambda i, j: (i // tiles_per_batch, 0, j)),
        ],
        out_specs=pl.BlockSpec((bm, bn), lambda i, j: (i, j)),
        compiler_params=_params(2),
        name="mm_res",
    )(x, w, res, gate)


def _softmax_pv(scores, values):
    m = functools.reduce(jnp.maximum, [jnp.max(s, axis=-1, keepdims=True) for s in scores])
    ps = [jnp.exp(s - m) for s in scores]
    denom = functools.reduce(jnp.add, [jnp.sum(p, axis=-1, keepdims=True) for p in ps])
    out = functools.reduce(
        jnp.add,
        [jnp.dot(p.astype(BF16), v, preferred_element_type=F32) for p, v in zip(ps, values)])
    return out / denom


def _qk(q, k):
    return lax.dot_general(q, k, (((1,), (1,)), ((), ())), preferred_element_type=F32)


def _na_attn_kernel(q_ref, k_ref, v_ref, qc_ref, kc_ref, vc_ref, bias_ref, o_ref, oc_ref,
                    *, rows, win_r, scale):
    kc = kc_ref[...]
    vc = vc_ref[...]

    def row_block(r, carry):
        r0 = jnp.clip(r - win_r // 2, 0, rows - win_r)
        q = q_ref[pl.ds(pl.multiple_of(r * GRID_W, GRID_W), GRID_W), :]
        k0 = pl.multiple_of(r0 * GRID_W, GRID_W)
        kw = k_ref[pl.ds(k0, win_r * GRID_W), :]
        vw = v_ref[pl.ds(k0, win_r * GRID_W), :]
        s_win = _qk(q, kw) * scale + bias_ref[r0 - r + (win_r - 1)]
        s_ctx = _qk(q, kc) * scale
        o = _softmax_pv([s_win, s_ctx], [vw, vc])
        o_ref[pl.ds(pl.multiple_of(r * GRID_W, GRID_W), GRID_W), :] = o.astype(BF16)
        return carry

    lax.fori_loop(0, rows, row_block, 0)
    oc_ref[...] = _softmax_pv([_qk(qc_ref[...], kc) * scale], [vc]).astype(BF16)


def _na_bias_table(rpb):
    n_heads, n_dr, n_dc = rpb.shape
    win_r, win_c = (n_dr + 1) // 2, (n_dc + 1) // 2
    cols = jnp.arange(GRID_W)
    c0 = jnp.clip(cols - win_c // 2, 0, GRID_W - win_c)
    col_in = (cols[None, :] >= c0[:, None]) & (cols[None, :] < c0[:, None] + win_c)
    dc = jnp.clip(cols[None, :] - cols[:, None] + win_c - 1, 0, 2 * win_c - 2)
    masked = jnp.where(col_in, rpb[:, :, dc].astype(F32), NEG_INF)
    dr = jnp.arange(win_r)[:, None] + jnp.arange(win_r)[None, :]
    tab = masked[:, dr]
    return tab.transpose(0, 1, 3, 2, 4).reshape(n_heads, win_r, GRID_W, win_r * GRID_W)


def _na_attention(qkv, qkvc, rpb, batch, seq, ctx_len, hd):
    n_heads = rpb.shape[0]
    win_r = (rpb.shape[1] + 1) // 2
    rows = seq // GRID_W
    assert rows >= win_r
    d = n_heads * hd
    bias = _na_bias_table(rpb)

    def spec(n_rows, part):
        return pl.BlockSpec((n_rows, hd), lambda h, b: (b, part * n_heads + h))

    return pl.pallas_call(
        functools.partial(_na_attn_kernel, rows=rows, win_r=win_r, scale=hd ** -0.5),
        out_shape=(jax.ShapeDtypeStruct((batch * seq, d), BF16),
                   jax.ShapeDtypeStruct((batch * ctx_len, d), BF16)),
        grid=(n_heads, batch),
        in_specs=[spec(seq, 0), spec(seq, 1), spec(seq, 2),
                  spec(ctx_len, 0), spec(ctx_len, 1), spec(ctx_len, 2),
                  pl.BlockSpec((None, win_r, GRID_W, win_r * GRID_W), lambda h, b: (h, 0, 0, 0))],
        out_specs=(pl.BlockSpec((seq, hd), lambda h, b: (b, h)),
                   pl.BlockSpec((ctx_len, hd), lambda h, b: (b, h))),
        compiler_params=_params(2),
        name="na_attention",
    )(qkv, qkv, qkv, qkvc, qkvc, qkvc, bias)


def _kv_outer(k, v):
    return lax.dot_general(k, v, (((0,), (0,)), ((), ())), preferred_element_type=F32)


def _retention_kernel(lgf_ref, lgb_ref, q_ref, k_ref, v_ref, g_ref, gnw_ref, *rest,
                      n_chunks, ctx_len):
    if ctx_len:
        kc_ref, vc_ref, o_ref, acc_ref, sf_ref, sb_ref = rest
    else:
        o_ref, acc_ref, sf_ref, sb_ref = rest
    c_len = RET_CHUNK
    h = pl.program_id(0)
    lgf = jnp.full((1, 1), lgf_ref[h], F32)
    lgb = jnp.full((1, 1), lgb_ref[h], F32)

    row = lax.broadcasted_iota(jnp.int32, (c_len, c_len), 0)
    col = lax.broadcasted_iota(jnp.int32, (c_len, c_len), 1)
    diff = (row - col).astype(F32)
    intra = jnp.where(diff >= 0, jnp.exp(lgf * jnp.maximum(diff, 0.0)),
                      jnp.exp(lgb * jnp.maximum(-diff, 0.0)))
    pos = lax.broadcasted_iota(jnp.int32, (c_len, 1), 0).astype(F32)
    q_dec_f = jnp.exp(lgf * (pos + 1.0))
    k_dec_f = jnp.exp(lgf * (c_len - 1.0 - pos))
    chunk_dec_f = jnp.exp(lgf * c_len)
    q_dec_b = jnp.exp(lgb * (c_len - pos))
    k_dec_b = jnp.exp(lgb * pos)
    chunk_dec_b = jnp.exp(lgb * c_len)

    if ctx_len:
        cpos = lax.broadcasted_iota(jnp.int32, (ctx_len, 1), 0).astype(F32)
        kc = kc_ref[...].astype(F32)
        vc = vc_ref[...]
        sf_ref[...] = _kv_outer((kc * jnp.exp(lgf * (ctx_len - 1.0 - cpos))).astype(BF16), vc)
        sb_ref[...] = _kv_outer((kc * jnp.exp(lgb * cpos)).astype(BF16), vc)
    else:
        sf_ref[...] = jnp.zeros_like(sf_ref)
        sb_ref[...] = jnp.zeros_like(sb_ref)

    def chunk_slice(c):
        return pl.ds(pl.multiple_of(c * c_len, c_len), c_len)

    def fwd_chunk(c, carry):
        sl = chunk_slice(c)
        q = q_ref[sl, :]
        k = k_ref[sl, :]
        v = v_ref[sl, :]
        s = (_qk(q, k) * intra).astype(BF16)
        qf = (q.astype(F32) * q_dec_f).astype(BF16)
        acc_ref[sl, :] = (jnp.dot(s, v, preferred_element_type=F32)
                          + jnp.dot(qf, sf_ref[...].astype(BF16), preferred_element_type=F32))
        kf = (k.astype(F32) * k_dec_f).astype(BF16)
        sf_ref[...] = sf_ref[...] * chunk_dec_f + _kv_outer(kf, v)
        return carry

    lax.fori_loop(0, n_chunks, fwd_chunk, 0)

    def bwd_chunk(t, carry):
        sl = chunk_slice(n_chunks - 1 - t)
        q = q_ref[sl, :]
        k = k_ref[sl, :]
        v = v_ref[sl, :]
        qb = (q.astype(F32) * q_dec_b).astype(BF16)
        o = acc_ref[sl, :] + jnp.dot(qb, sb_ref[...].astype(BF16), preferred_element_type=F32)
        kb = (k.astype(F32) * k_dec_b).astype(BF16)
        sb_ref[...] = sb_ref[...] * chunk_dec_b + _kv_outer(kb, v)
        mu = jnp.mean(o, axis=-1, keepdims=True)
        cen = o - mu
        var = jnp.mean(cen * cen, axis=-1, keepdims=True)
        normed = cen * lax.rsqrt(var + EPS)
        o_ref[sl, :] = (normed * gnw_ref[...] * g_ref[sl, :].astype(F32)).astype(BF16)
        return carry

    lax.fori_loop(0, n_chunks, bwd_chunk, 0)


def _retention(qkvg, lg_f, lg_b, gn_w, batch, n_tok, hd, qkvg_ctx=None, ctx_len=0):
    n_heads = lg_f.shape[0]
    d = n_heads * hd
    n_chunks = n_tok // RET_CHUNK

    def spec(n_rows, part):
        return pl.BlockSpec((n_rows, hd), lambda h, b: (b, part * n_heads + h))

    smem = pl.BlockSpec(memory_space=pltpu.SMEM)
    in_specs = [smem, smem, spec(n_tok, 0), spec(n_tok, 1), spec(n_tok, 2), spec(n_tok, 3),
                pl.BlockSpec((1, hd), lambda h, b: (0, h))]
    args = [lg_f, lg_b, qkvg, qkvg, qkvg, qkvg, gn_w.reshape(1, d)]
    if ctx_len:
        in_specs += [spec(ctx_len, 1), spec(ctx_len, 2)]
        args += [qkvg_ctx, qkvg_ctx]
    return pl.pallas_call(
        functools.partial(_retention_kernel, n_chunks=n_chunks, ctx_len=ctx_len),
        out_shape=jax.ShapeDtypeStruct((batch * n_tok, d), BF16),
        grid=(n_heads, batch),
        in_specs=in_specs,
        out_specs=pl.BlockSpec((n_tok, hd), lambda h, b: (b, h)),
        scratch_shapes=[pltpu.VMEM((n_tok, hd), F32),
                        pltpu.VMEM((hd, hd), F32),
                        pltpu.VMEM((hd, hd), F32)],
        compiler_params=_params(2),
        name="retention",
    )(*args)


def _rope_tables(seq, hd):
    t = jnp.arange(seq)
    row = (t // GRID_W).astype(F32)
    col = (t % GRID_W).astype(F32)
    axis_dim = hd // 2
    inv_freq = jnp.power(ROPE_BASE, -jnp.arange(0, axis_dim, 2, dtype=F32) / axis_dim)
    ang_r = row[:, None] * inv_freq[None, :]
    ang_c = col[:, None] * inv_freq[None, :]
    cos = jnp.concatenate([jnp.cos(ang_r), jnp.cos(ang_r), jnp.cos(ang_c), jnp.cos(ang_c)], axis=-1)
    sin = jnp.concatenate([-jnp.sin(ang_r), jnp.sin(ang_r), -jnp.sin(ang_c), jnp.sin(ang_c)], axis=-1)
    return cos, sin


def kernel(x, c, ctx, c_ctx, ada_w, ada_b, norm1_w, norm2_w, na_wqkv, na_wo, na_qnorm_w, na_knorm_w, na_rpb, ret_wqkvg, ret_wo, ret_decay_f, ret_decay_b, ret_gn_w, ffn_w13, ffn_w2, moe_router, moe_w13, moe_w2):
    batch, seq, d = x.shape
    ctx_len = ctx.shape[1]
    depth = ada_w.shape[0]
    na_hd = na_qnorm_w.shape[1]
    ret_heads = ret_decay_f.shape[1]
    ret_hd = d // ret_heads
    assert batch + 1 <= ADA_ROWS

    cond = jnp.zeros((ADA_ROWS, d), F32).at[:batch].set(c).at[batch].set(c_ctx)
    ada = _ada_params(cond, ada_w, ada_b)
    rope = _rope_tables(seq, ret_hd)

    h = x.reshape(batch * seq, d)
    hc = ctx.reshape(batch * ctx_len, d)
    for i in range(depth):
        j = i // 2
        ctx_out = i < depth - 1
        mods = ada[i].reshape(ADA_ROWS, 6, 1, d)
        sh1, sc1, g1, sh2, sc2, g2 = (mods[:batch, p] for p in range(6))
        csh1, csc1, cg1, csh2, csc2, cg2 = (mods[batch:batch + 1, p] for p in range(6))

        u = _norm_mod(h, norm1_w[i], sh1, sc1, seq)
        uc = _norm_mod(hc, norm1_w[i], csh1, csc1, batch * ctx_len)
        if i % 2 == 0:
            w_qkv = na_wqkv[j].astype(BF16)
            w_o = na_wo[j].astype(BF16)
            qkv = _mm_na_qkv(u, w_qkv, na_qnorm_w[j], na_knorm_w[j])
            qkvc = _mm_na_qkv(uc, w_qkv, na_qnorm_w[j], na_knorm_w[j])
            o, oc = _na_attention(qkv, qkvc, na_rpb[j], batch, seq, ctx_len, na_hd)
        else:
            w_qkvg = ret_wqkvg[j].astype(BF16)
            w_o = ret_wo[j].astype(BF16)
            lg_f = jax.nn.log_sigmoid(ret_decay_f[j].astype(F32))
            lg_b = jax.nn.log_sigmoid(ret_decay_b[j].astype(F32))
            qkvg = _mm_ret_qkvg(u, w_qkvg, ret_hd, rope, seq)
            qkvgc = _mm_ret_qkvg(uc, w_qkvg, ret_hd, None, batch * ctx_len)
            o = _retention(qkvg, lg_f, lg_b, ret_gn_w[j], batch, seq, ret_hd, qkvgc, ctx_len)
            if ctx_out:
                oc = _retention(qkvgc, lg_f, lg_b, ret_gn_w[j], batch, ctx_len, ret_hd)
        h = _mm_res(o, w_o, h, g1, seq)
        if ctx_out:
            hc = _mm_res(oc, w_o, hc, cg1, batch * ctx_len)

        if i % 2 == 0:
            w13 = ffn_w13[j].astype(BF16)
            w2 = ffn_w2[j].astype(BF16)
            u2 = _norm_mod(h, norm2_w[i], sh2, sc2, seq)
            h = _mm_res(_mm_swiglu(u2, w13), w2, h, g2, seq)
            if ctx_out:
                uc2 = _norm_mod(hc, norm2_w[i], csh2, csc2, batch * ctx_len)
                hc = _mm_res(_mm_swiglu(uc2, w13), w2, hc, cg2, batch * ctx_len)
        else:
            w13 = moe_w13[j].astype(BF16)
            w2 = moe_w2[j].astype(BF16).reshape(-1, d)
            u2, comb = _norm_mod(h, norm2_w[i], sh2, sc2, seq, moe_router[j])
            h = _mm_res(_mm_swiglu(u2, w13, comb), w2, h, g2, seq)
            if ctx_out:
                uc2, combc = _norm_mod(hc, norm2_w[i], csh2, csc2, batch * ctx_len, moe_router[j])
                hc = _mm_res(_mm_swiglu(uc2, w13, combc), w2, hc, cg2, batch * ctx_len)
    return h.reshape(batch, seq, d)
```

```python
import functools

import jax
import jax.numpy as jnp
from jax import lax
from jax.experimental import pallas as pl
from jax.experimental.pallas import tpu as pltpu

F32 = jnp.float32
BF16 = jnp.bfloat16

GRID_W = 64
RET_CHUNK = 128
ROPE_BASE = 10000.0
EPS = 1e-6
NEG_INF = -1e30

V7X_VMEM_BYTES = 64 * 1024 * 1024
VMEM_LIMIT_BYTES = V7X_VMEM_BYTES * 7 // 8
LANES = 128
ADA_ROWS = 8


def _params(n_grid_dims):
    return pltpu.CompilerParams(
        dimension_semantics=("arbitrary",) * n_grid_dims,
        vmem_limit_bytes=VMEM_LIMIT_BYTES,
    )


def _tile(n, pref):
    t = min(pref, n)
    while n % t:
        t //= 2
    return t


def _silu(x):
    return x * (1.0 / (1.0 + jnp.exp(-x)))


def _ada_kernel(x_ref, w_ref, b_ref, o_ref):
    xa = _silu(x_ref[...]).astype(BF16)
    w = w_ref[...].astype(BF16)
    o_ref[...] = jnp.dot(xa, w, preferred_element_type=F32) + b_ref[...]


def _ada_params(cond, ada_w, ada_b):
    depth, d, n = ada_w.shape
    bn = _tile(n, 512)
    return pl.pallas_call(
        _ada_kernel,
        out_shape=jax.ShapeDtypeStruct((depth, ADA_ROWS, n), F32),
        grid=(depth, n // bn),
        in_specs=[
            pl.BlockSpec((ADA_ROWS, d), lambda l, j: (0, 0)),
            pl.BlockSpec((None, d, bn), lambda l, j: (l, 0, j)),
            pl.BlockSpec((None, 1, bn), lambda l, j: (l, 0, j)),
        ],
        out_specs=pl.BlockSpec((None, ADA_ROWS, bn), lambda l, j: (l, 0, j)),
        compiler_params=_params(2),
        name="ada_params",
    )(cond, ada_w, ada_b.reshape(depth, 1, n))


def _norm_mod_kernel(x_ref, nw_ref, sh_ref, sc_ref, *rest, n_experts):
    x = x_ref[...]
    y = x * lax.rsqrt(jnp.mean(x * x, axis=-1, keepdims=True) + EPS)
    u = (y * nw_ref[...]) * (1.0 + sc_ref[...]) + sh_ref[...]
    ub = u.astype(BF16)
    if n_experts == 0:
        (u_ref,) = rest
        u_ref[...] = ub
        return
    wr_ref, u_ref, comb_ref = rest
    u_ref[...] = ub
    logits = jnp.dot(ub, wr_ref[...], preferred_element_type=F32)
    lane = lax.broadcasted_iota(jnp.int32, logits.shape, 1).astype(F32)
    valid = lane < n_experts
    logits = jnp.where(valid, logits, NEG_INF)
    e = jnp.exp(logits - jnp.max(logits, axis=-1, keepdims=True))
    probs = jnp.where(valid, e / jnp.sum(e, axis=-1, keepdims=True), -1.0)
    p1 = jnp.max(probs, axis=-1, keepdims=True)
    i1 = jnp.min(jnp.where(probs == p1, lane, float(LANES)), axis=-1, keepdims=True)
    rest_p = jnp.where(lane == i1, -1.0, probs)
    p2 = jnp.max(rest_p, axis=-1, keepdims=True)
    i2 = jnp.min(jnp.where(rest_p == p2, lane, float(LANES)), axis=-1, keepdims=True)
    inv = 1.0 / (p1 + p2)
    comb_ref[...] = jnp.where(lane == i1, p1 * inv, jnp.where(lane == i2, p2 * inv, 0.0))


def _norm_mod(x, norm_w, shift, scale, rows_per_batch, router_w=None):
    m, d = x.shape
    bm = _tile(rows_per_batch, 256)
    tiles_per_batch = rows_per_batch // bm
    mod_spec = pl.BlockSpec((None, 1, d), lambda i: (i // tiles_per_batch, 0, 0))
    in_specs = [
        pl.BlockSpec((bm, d), lambda i: (i, 0)),
        pl.BlockSpec((1, d), lambda i: (0, 0)),
        mod_spec,
        mod_spec,
    ]
    args = [x, norm_w.reshape(1, d), shift, scale]
    u_shape = jax.ShapeDtypeStruct((m, d), BF16)
    u_spec = pl.BlockSpec((bm, d), lambda i: (i, 0))
    if router_w is None:
        n_experts = 0
        out_shape, out_specs = u_shape, u_spec
    else:
        n_experts = router_w.shape[1]
        wr = jnp.zeros((d, LANES), BF16).at[:, :n_experts].set(router_w.astype(BF16))
        in_specs.append(pl.BlockSpec((d, LANES), lambda i: (0, 0)))
        args.append(wr)
        out_shape = (u_shape, jax.ShapeDtypeStruct((m, LANES), F32))
        out_specs = (u_spec, pl.BlockSpec((bm, LANES), lambda i: (i, 0)))
    return pl.pallas_call(
        functools.partial(_norm_mod_kernel, n_experts=n_experts),
        out_shape=out_shape,
        grid=(m // bm,),
        in_specs=in_specs,
        out_specs=out_specs,
        compiler_params=_params(1),
        name="norm_mod",
    )(*args)


def _mm_na_qkv_kernel(x_ref, w_ref, qn_ref, kn_ref, o_ref, *, tiles_per_part, hd):
    j = pl.program_id(1)
    acc = jnp.dot(x_ref[...], w_ref[...], preferred_element_type=F32)
    n_heads = acc.shape[1] // hd

    def store_normed(nw):
        for t in range(n_heads):
            a = acc[:, t * hd:(t + 1) * hd]
            y = a * lax.rsqrt(jnp.mean(a * a, axis=-1, keepdims=True) + EPS)
            o_ref[:, t * hd:(t + 1) * hd] = (y * nw).astype(BF16)

    @pl.when(j < tiles_per_part)
    def _():
        store_normed(qn_ref[...])

    @pl.when((j >= tiles_per_part) & (j < 2 * tiles_per_part))
    def _():
        store_normed(kn_ref[...])

    @pl.when(j >= 2 * tiles_per_part)
    def _():
        o_ref[...] = acc.astype(BF16)


def _mm_na_qkv(x, w, qn_w, kn_w):
    m, k = x.shape
    n = w.shape[1]
    hd = qn_w.shape[0]
    bm, bn = _tile(m, 1024), _tile(n // 3, 1024)
    return pl.pallas_call(
        functools.partial(_mm_na_qkv_kernel, tiles_per_part=n // 3 // bn, hd=hd),
        out_shape=jax.ShapeDtypeStruct((m, n), BF16),
        grid=(m // bm, n // bn),
        in_specs=[
            pl.BlockSpec((bm, k), lambda i, j: (i, 0)),
            pl.BlockSpec((k, bn), lambda i, j: (0, j)),
            pl.BlockSpec((1, hd), lambda i, j: (0, 0)),
            pl.BlockSpec((1, hd), lambda i, j: (0, 0)),
        ],
        out_specs=pl.BlockSpec((bm, bn), lambda i, j: (i, j)),
        compiler_params=_params(2),
        name="mm_na_qkv",
    )(x, w, qn_w.reshape(1, hd), kn_w.reshape(1, hd))


def _mm_ret_qkvg_kernel(x_ref, w_ref, *rest, tiles_per_part, hd, rope):
    if rope:
        cos_ref, sin_ref, o_ref = rest
    else:
        (o_ref,) = rest
    j = pl.program_id(1)
    acc = jnp.dot(x_ref[...], w_ref[...], preferred_element_type=F32)
    n_heads = acc.shape[1] // hd
    k_scale = hd ** -0.5

    def store_qk(scale):
        for t in range(n_heads):
            a = acc[:, t * hd:(t + 1) * hd]
            if rope:
                lane = lax.broadcasted_iota(jnp.int32, a.shape, 1)
                first = (lane % (hd // 2)) < (hd // 4)
                partner = jnp.where(first, pltpu.roll(a, hd - hd // 4, 1), pltpu.roll(a, hd // 4, 1))
                a = a * cos_ref[...] + partner * sin_ref[...]
            o_ref[:, t * hd:(t + 1) * hd] = (a * scale).astype(BF16)

    @pl.when(j < tiles_per_part)
    def _():
        store_qk(1.0)

    @pl.when((j >= tiles_per_part) & (j < 2 * tiles_per_part))
    def _():
        store_qk(k_scale)

    @pl.when((j >= 2 * tiles_per_part) & (j < 3 * tiles_per_part))
    def _():
        o_ref[...] = acc.astype(BF16)

    @pl.when(j >= 3 * tiles_per_part)
    def _():
        o_ref[...] = _silu(acc).astype(BF16)


def _mm_ret_qkvg(x, w, hd, rope_tables, seq):
    m, k = x.shape
    n = w.shape[1]
    bm, bn = _tile(seq, 1024), _tile(n // 4, 1024)
    rope = rope_tables is not None
    in_specs = [
        pl.BlockSpec((bm, k), lambda i, j: (i, 0)),
        pl.BlockSpec((k, bn), lambda i, j: (0, j)),
    ]
    args = [x, w]
    if rope:
        tiles_per_seq = seq // bm
        tab_spec = pl.BlockSpec((bm, hd), lambda i, j: (i % tiles_per_seq, 0))
        in_specs += [tab_spec, tab_spec]
        args += list(rope_tables)
    return pl.pallas_call(
        functools.partial(_mm_ret_qkvg_kernel, tiles_per_part=n // 4 // bn, hd=hd, rope=rope),
        out_shape=jax.ShapeDtypeStruct((m, n), BF16),
        grid=(m // bm, n // bn),
        in_specs=in_specs,
        out_specs=pl.BlockSpec((bm, bn), lambda i, j: (i, j)),
        compiler_params=_params(2),
        name="mm_ret_qkvg",
    )(*args)


def _mm_swiglu_kernel(x_ref, wa_ref, wb_ref, *rest, tiles_per_expert):
    x = x_ref[...]
    a = jnp.dot(x, wa_ref[...], preferred_element_type=F32)
    b = jnp.dot(x, wb_ref[...], preferred_element_type=F32)
    act = _silu(a) * b
    if tiles_per_expert:
        comb_ref, o_ref = rest
        e = pl.program_id(1) // tiles_per_expert
        comb = comb_ref[...]
        lane = lax.broadcasted_iota(jnp.int32, comb.shape, 1)
        act = act * jnp.sum(jnp.where(lane == e, comb, 0.0), axis=-1, keepdims=True)
    else:
        (o_ref,) = rest
    o_ref[...] = act.astype(BF16)


def _mm_swiglu(x, w13, comb=None):
    m, k = x.shape
    bm = _tile(m, 1024)
    if comb is None:
        f = w13.shape[1] // 2
        bn = _tile(f, 512)
        nt = f // bn
        wa_spec = pl.BlockSpec((k, bn), lambda i, j: (0, j))
        wb_spec = pl.BlockSpec((k, bn), lambda i, j: (0, j + nt))
        in_specs = [pl.BlockSpec((bm, k), lambda i, j: (i, 0)), wa_spec, wb_spec]
        args = [x, w13, w13]
        tpe = 0
        n_out = f
    else:
        n_e, _, ed2 = w13.shape
        ed = ed2 // 2
        bn = _tile(ed, 512)
        tpe = ed // bn
        wa_spec = pl.BlockSpec((None, k, bn), lambda i, j: (j // tpe, 0, j % tpe))
        wb_spec = pl.BlockSpec((None, k, bn), lambda i, j: (j // tpe, 0, tpe + j % tpe))
        in_specs = [pl.BlockSpec((bm, k), lambda i, j: (i, 0)), wa_spec, wb_spec,
                    pl.BlockSpec((bm, LANES), lambda i, j: (i, 0))]
        args = [x, w13, w13, comb]
        n_out = n_e * ed
    return pl.pallas_call(
        functools.partial(_mm_swiglu_kernel, tiles_per_expert=tpe),
        out_shape=jax.ShapeDtypeStruct((m, n_out), BF16),
        grid=(m // bm, n_out // bn),
        in_specs=in_specs,
        out_specs=pl.BlockSpec((bm, bn), lambda i, j: (i, j)),
        compiler_params=_params(2),
        name="mm_swiglu",
    )(*args)


def _mm_res_kernel(x_ref, w_ref, res_ref, gate_ref, o_ref):
    acc = jnp.dot(x_ref[...], w_ref[...], preferred_element_type=F32)
    o_ref[...] = res_ref[...] + gate_ref[...] * acc


def _mm_res(x, w, res, gate, rows_per_batch):
    m, k = x.shape
    n = w.shape[1]
    bm, bn = _tile(rows_per_batch, 1024), _tile(n, 512)
    tiles_per_batch = rows_per_batch // bm
    return pl.pallas_call(
        _mm_res_kernel,
        out_shape=jax.ShapeDtypeStruct((m, n), F32),
        grid=(m // bm, n // bn),
        in_specs=[
            pl.BlockSpec((bm, k), lambda i, j: (i, 0)),
            pl.BlockSpec((k, bn), lambda i, j: (0, j)),
            pl.BlockSpec((bm, bn), lambda i, j: (i, j)),
            pl.BlockSpec((None, 1, bn), lambda i, j: (i // tiles_per_batch, 0, j)),
        ],
        out_specs=pl.BlockSpec((bm, bn), lambda i, j: (i, j)),
        compiler_params=_params(2),
        name="mm_res",
    )(x, w, res, gate)


def _softmax_pv(scores, values):
    m = functools.reduce(jnp.maximum, [jnp.max(s, axis=-1, keepdims=True) for s in scores])
    ps = [jnp.exp(s - m) for s in scores]
    denom = functools.reduce(jnp.add, [jnp.sum(p, axis=-1, keepdims=True) for p in ps])
    out = functools.reduce(
        jnp.add,
        [jnp.dot(p.astype(BF16), v, preferred_element_type=F32) for p, v in zip(ps, values)])
    return out / denom


def _qk(q, k):
    return lax.dot_general(q, k, (((1,), (1,)), ((), ())), preferred_element_type=F32)


NA_Q_ROWS = 4
NA_KEY_ROWS = 12


def _na_attn_kernel(q_ref, k_ref, v_ref, qc_ref, kc_ref, vc_ref, bias_ref, o_ref, oc_ref,
                    *, rows, win_r, scale):
    kc = kc_ref[...]
    vc = vc_ref[...]
    n_blk = rows // NA_Q_ROWS
    n_q = NA_Q_ROWS * GRID_W
    n_k = NA_KEY_ROWS * GRID_W

    def query_block(t, carry):
        u0 = jnp.clip(t * NA_Q_ROWS - win_r // 2, 0, rows - NA_KEY_ROWS)
        slab = jnp.where(t == 0, 0, jnp.where(t == n_blk - 1, 2, 1))
        q_rows = pl.ds(pl.multiple_of(t * n_q, n_q), n_q)
        k_rows = pl.ds(pl.multiple_of(u0 * GRID_W, GRID_W), n_k)
        q = q_ref[q_rows, :]
        s_win = _qk(q, k_ref[k_rows, :]) * scale + bias_ref[slab]
        s_ctx = _qk(q, kc) * scale
        o = _softmax_pv([s_win, s_ctx], [v_ref[k_rows, :], vc])
        o_ref[q_rows, :] = o.astype(BF16)
        return carry

    lax.fori_loop(0, n_blk, query_block, 0, unroll=2)
    oc_ref[...] = _softmax_pv([_qk(qc_ref[...], kc) * scale], [vc]).astype(BF16)


def _na_bias_table(rpb, rows):
    n_heads, n_dr, n_dc = rpb.shape
    win_r, win_c = (n_dr + 1) // 2, (n_dc + 1) // 2
    n_blk = rows // NA_Q_ROWS
    assert rows % NA_Q_ROWS == 0 and n_blk >= 3 and NA_Q_ROWS >= win_r // 2
    assert (n_blk - 2) * NA_Q_ROWS - win_r // 2 <= rows - NA_KEY_ROWS
    assert NA_KEY_ROWS >= win_r + NA_Q_ROWS - 1
    cols = jnp.arange(GRID_W)
    c0 = jnp.clip(cols - win_c // 2, 0, GRID_W - win_c)
    col_in = (cols[None, :] >= c0[:, None]) & (cols[None, :] < c0[:, None] + win_c)
    dc = jnp.clip(cols[None, :] - cols[:, None] + win_c - 1, 0, 2 * win_c - 2)
    masked = jnp.where(col_in, rpb[:, :, dc].astype(F32), NEG_INF)
    slabs = []
    for t in (0, 1, n_blk - 1):
        u0 = min(max(t * NA_Q_ROWS - win_r // 2, 0), rows - NA_KEY_ROWS)
        q_row = t * NA_Q_ROWS + jnp.arange(NA_Q_ROWS)
        r0 = jnp.clip(q_row - win_r // 2, 0, rows - win_r)
        k_row = u0 + jnp.arange(NA_KEY_ROWS)
        in_win = (k_row[None, :] >= r0[:, None]) & (k_row[None, :] < r0[:, None] + win_r)
        dr = jnp.clip(k_row[None, :] - q_row[:, None] + win_r - 1, 0, 2 * win_r - 2)
        slab = jnp.where(in_win[None, :, :, None, None], masked[:, dr], NEG_INF)
        slabs.append(slab.transpose(0, 1, 3, 2, 4).reshape(
            n_heads, NA_Q_ROWS * GRID_W, NA_KEY_ROWS * GRID_W))
    return jnp.stack(slabs, axis=1)


def _na_attention(qkv, qkvc, rpb, batch, seq, ctx_len, hd):
    n_heads = rpb.shape[0]
    win_r = (rpb.shape[1] + 1) // 2
    rows = seq // GRID_W
    d = n_heads * hd
    bias = _na_bias_table(rpb, rows)

    def spec(n_rows, part):
        return pl.BlockSpec((n_rows, hd), lambda h, b: (b, part * n_heads + h))

    return pl.pallas_call(
        functools.partial(_na_attn_kernel, rows=rows, win_r=win_r, scale=hd ** -0.5),
        out_shape=(jax.ShapeDtypeStruct((batch * seq, d), BF16),
                   jax.ShapeDtypeStruct((batch * ctx_len, d), BF16)),
        grid=(n_heads, batch),
        in_specs=[spec(seq, 0), spec(seq, 1), spec(seq, 2),
                  spec(ctx_len, 0), spec(ctx_len, 1), spec(ctx_len, 2),
                  pl.BlockSpec((None,) + bias.shape[1:], lambda h, b: (h, 0, 0, 0))],
        out_specs=(pl.BlockSpec((seq, hd), lambda h, b: (b, h)),
                   pl.BlockSpec((ctx_len, hd), lambda h, b: (b, h))),
        compiler_params=_params(2),
        name="na_attention",
    )(qkv, qkv, qkv, qkvc, qkvc, qkvc, bias)


def _kv_outer(k, v):
    return lax.dot_general(k, v, (((0,), (0,)), ((), ())), preferred_element_type=F32)


def _retention_kernel(lgf_ref, lgb_ref, q_ref, k_ref, v_ref, g_ref, gnw_ref, *rest,
                      n_chunks, ctx_len):
    if ctx_len:
        kc_ref, vc_ref, o_ref, acc_ref, sf_ref, sb_ref = rest
    else:
        o_ref, acc_ref, sf_ref, sb_ref = rest
    c_len = RET_CHUNK
    h = pl.program_id(0)
    lgf = jnp.full((1, 1), lgf_ref[h], F32)
    lgb = jnp.full((1, 1), lgb_ref[h], F32)

    row = lax.broadcasted_iota(jnp.int32, (c_len, c_len), 0)
    col = lax.broadcasted_iota(jnp.int32, (c_len, c_len), 1)
    diff = (row - col).astype(F32)
    intra = jnp.where(diff >= 0, jnp.exp(lgf * jnp.maximum(diff, 0.0)),
                      jnp.exp(lgb * jnp.maximum(-diff, 0.0)))
    pos = lax.broadcasted_iota(jnp.int32, (c_len, 1), 0).astype(F32)
    q_dec_f = jnp.exp(lgf * (pos + 1.0))
    k_dec_f = jnp.exp(lgf * (c_len - 1.0 - pos))
    chunk_dec_f = jnp.exp(lgf * c_len)
    q_dec_b = jnp.exp(lgb * (c_len - pos))
    k_dec_b = jnp.exp(lgb * pos)
    chunk_dec_b = jnp.exp(lgb * c_len)

    if ctx_len:
        cpos = lax.broadcasted_iota(jnp.int32, (ctx_len, 1), 0).astype(F32)
        kc = kc_ref[...].astype(F32)
        vc = vc_ref[...]
        sf_ref[...] = _kv_outer((kc * jnp.exp(lgf * (ctx_len - 1.0 - cpos))).astype(BF16), vc)
        sb_ref[...] = _kv_outer((kc * jnp.exp(lgb * cpos)).astype(BF16), vc)
    else:
        sf_ref[...] = jnp.zeros_like(sf_ref)
        sb_ref[...] = jnp.zeros_like(sb_ref)

    def chunk_slice(c):
        return pl.ds(pl.multiple_of(c * c_len, c_len), c_len)

    def fwd_part(sl):
        q = q_ref[sl, :]
        k = k_ref[sl, :]
        v = v_ref[sl, :]
        s = (_qk(q, k) * intra).astype(BF16)
        qf = (q.astype(F32) * q_dec_f).astype(BF16)
        val = (jnp.dot(s, v, preferred_element_type=F32)
               + jnp.dot(qf, sf_ref[...].astype(BF16), preferred_element_type=F32))
        kf = (k.astype(F32) * k_dec_f).astype(BF16)
        sf_ref[...] = sf_ref[...] * chunk_dec_f + _kv_outer(kf, v)
        return val

    def bwd_part(sl):
        q = q_ref[sl, :]
        k = k_ref[sl, :]
        v = v_ref[sl, :]
        qb = (q.astype(F32) * q_dec_b).astype(BF16)
        val = jnp.dot(qb, sb_ref[...].astype(BF16), preferred_element_type=F32)
        kb = (k.astype(F32) * k_dec_b).astype(BF16)
        sb_ref[...] = sb_ref[...] * chunk_dec_b + _kv_outer(kb, v)
        return val

    def finish(sl, o):
        mu = jnp.mean(o, axis=-1, keepdims=True)
        cen = o - mu
        var = jnp.mean(cen * cen, axis=-1, keepdims=True)
        normed = cen * lax.rsqrt(var + EPS)
        o_ref[sl, :] = (normed * gnw_ref[...] * g_ref[sl, :].astype(F32)).astype(BF16)

    half = n_chunks // 2

    def first_half(t, carry):
        lo, hi = chunk_slice(t), chunk_slice(n_chunks - 1 - t)
        acc_ref[lo, :] = fwd_part(lo)
        acc_ref[hi, :] = bwd_part(hi)
        return carry

    def second_half(t, carry):
        hi, lo = chunk_slice(half + t), chunk_slice(half - 1 - t)
        finish(hi, acc_ref[hi, :] + fwd_part(hi))
        finish(lo, acc_ref[lo, :] + bwd_part(lo))
        return carry

    unroll = 2 if half % 2 == 0 else 1
    lax.fori_loop(0, half, first_half, 0, unroll=unroll)
    lax.fori_loop(0, half, second_half, 0, unroll=unroll)


def _retention(qkvg, lg_f, lg_b, gn_w, batch, n_tok, hd, qkvg_ctx=None, ctx_len=0):
    n_heads = lg_f.shape[0]
    d = n_heads * hd
    n_chunks = n_tok // RET_CHUNK
    assert n_tok % (2 * RET_CHUNK) == 0

    def spec(n_rows, part):
        return pl.BlockSpec((n_rows, hd), lambda h, b: (b, part * n_heads + h))

    smem = pl.BlockSpec(memory_space=pltpu.SMEM)
    in_specs = [smem, smem, spec(n_tok, 0), spec(n_tok, 1), spec(n_tok, 2), spec(n_tok, 3),
                pl.BlockSpec((1, hd), lambda h, b: (0, h))]
    args = [lg_f, lg_b, qkvg, qkvg, qkvg, qkvg, gn_w.reshape(1, d)]
    if ctx_len:
        in_specs += [spec(ctx_len, 1), spec(ctx_len, 2)]
        args += [qkvg_ctx, qkvg_ctx]
    return pl.pallas_call(
        functools.partial(_retention_kernel, n_chunks=n_chunks, ctx_len=ctx_len),
        out_shape=jax.ShapeDtypeStruct((batch * n_tok, d), BF16),
        grid=(n_heads, batch),
        in_specs=in_specs,
        out_specs=pl.BlockSpec((n_tok, hd), lambda h, b: (b, h)),
        scratch_shapes=[pltpu.VMEM((n_tok, hd), F32),
                        pltpu.VMEM((hd, hd), F32),
                        pltpu.VMEM((hd, hd), F32)],
        compiler_params=_params(2),
        name="retention",
    )(*args)


def _rope_tables(seq, hd):
    t = jnp.arange(seq)
    row = (t // GRID_W).astype(F32)
    col = (t % GRID_W).astype(F32)
    axis_dim = hd // 2
    inv_freq = jnp.power(ROPE_BASE, -jnp.arange(0, axis_dim, 2, dtype=F32) / axis_dim)
    ang_r = row[:, None] * inv_freq[None, :]
    ang_c = col[:, None] * inv_freq[None, :]
    cos = jnp.concatenate([jnp.cos(ang_r), jnp.cos(ang_r), jnp.cos(ang_c), jnp.cos(ang_c)], axis=-1)
    sin = jnp.concatenate([-jnp.sin(ang_r), jnp.sin(ang_r), -jnp.sin(ang_c), jnp.sin(ang_c)], axis=-1)
    return cos, sin


def kernel(x, c, ctx, c_ctx, ada_w, ada_b, norm1_w, norm2_w, na_wqkv, na_wo, na_qnorm_w, na_knorm_w, na_rpb, ret_wqkvg, ret_wo, ret_decay_f, ret_decay_b, ret_gn_w, ffn_w13, ffn_w2, moe_router, moe_w13, moe_w2):
    batch, seq, d = x.shape
    ctx_len = ctx.shape[1]
    depth = ada_w.shape[0]
    na_hd = na_qnorm_w.shape[1]
    ret_heads = ret_decay_f.shape[1]
    ret_hd = d // ret_heads
    assert batch + 1 <= ADA_ROWS

    cond = jnp.zeros((ADA_ROWS, d), F32).at[:batch].set(c).at[batch].set(c_ctx)
    ada = _ada_params(cond, ada_w, ada_b)
    rope = _rope_tables(seq, ret_hd)

    h = x.reshape(batch * seq, d)
    hc = ctx.reshape(batch * ctx_len, d)
    for i in range(depth):
        j = i // 2
        ctx_out = i < depth - 1
        mods = ada[i].reshape(ADA_ROWS, 6, 1, d)
        sh1, sc1, g1, sh2, sc2, g2 = (mods[:batch, p] for p in range(6))
        csh1, csc1, cg1, csh2, csc2, cg2 = (mods[batch:batch + 1, p] for p in range(6))

        u = _norm_mod(h, norm1_w[i], sh1, sc1, seq)
        uc = _norm_mod(hc, norm1_w[i], csh1, csc1, batch * ctx_len)
        if i % 2 == 0:
            w_qkv = na_wqkv[j].astype(BF16)
            w_o = na_wo[j].astype(BF16)
            qkv = _mm_na_qkv(u, w_qkv, na_qnorm_w[j], na_knorm_w[j])
            qkvc = _mm_na_qkv(uc, w_qkv, na_qnorm_w[j], na_knorm_w[j])
            o, oc = _na_attention(qkv, qkvc, na_rpb[j], batch, seq, ctx_len, na_hd)
        else:
            w_qkvg = ret_wqkvg[j].astype(BF16)
            w_o = ret_wo[j].astype(BF16)
            lg_f = jax.nn.log_sigmoid(ret_decay_f[j].astype(F32))
            lg_b = jax.nn.log_sigmoid(ret_decay_b[j].astype(F32))
            qkvg = _mm_ret_qkvg(u, w_qkvg, ret_hd, rope, seq)
            qkvgc = _mm_ret_qkvg(uc, w_qkvg, ret_hd, None, batch * ctx_len)
            o = _retention(qkvg, lg_f, lg_b, ret_gn_w[j], batch, seq, ret_hd, qkvgc, ctx_len)
            if ctx_out:
                oc = _retention(qkvgc, lg_f, lg_b, ret_gn_w[j], batch, ctx_len, ret_hd)
        h = _mm_res(o, w_o, h, g1, seq)
        if ctx_out:
            hc = _mm_res(oc, w_o, hc, cg1, batch * ctx_len)

        if i % 2 == 0:
            w13 = ffn_w13[j].astype(BF16)
            w2 = ffn_w2[j].astype(BF16)
            u2 = _norm_mod(h, norm2_w[i], sh2, sc2, seq)
            h = _mm_res(_mm_swiglu(u2, w13), w2, h, g2, seq)
            if ctx_out:
                uc2 = _norm_mod(hc, norm2_w[i], csh2, csc2, batch * ctx_len)
                hc = _mm_res(_mm_swiglu(uc2, w13), w2, hc, cg2, batch * ctx_len)
        else:
            w13 = moe_w13[j].astype(BF16)
            w2 = moe_w2[j].astype(BF16).reshape(-1, d)
            u2, comb = _norm_mod(h, norm2_w[i], sh2, sc2, seq, moe_router[j])
            h = _mm_res(_mm_swiglu(u2, w13, comb), w2, h, g2, seq)
            if ctx_out:
                uc2, combc = _norm_mod(hc, norm2_w[i], csh2, csc2, batch * ctx_len, moe_router[j])
                hc = _mm_res(_mm_swiglu(uc2, w13, combc), w2, hc, cg2, batch * ctx_len)
    return h.reshape(batch, seq, d)
```

```python
import functools

import jax
import jax.numpy as jnp
from jax import lax
from jax.experimental import pallas as pl
from jax.experimental.pallas import tpu as pltpu

F32 = jnp.float32
BF16 = jnp.bfloat16

GRID_W = 64
RET_CHUNK = 128
ROPE_BASE = 10000.0
EPS = 1e-6
NEG_INF = -1e30

V7X_VMEM_BYTES = 64 * 1024 * 1024
VMEM_LIMIT_BYTES = V7X_VMEM_BYTES * 7 // 8
LANES = 128
ADA_ROWS = 8


def _params(n_grid_dims):
    return pltpu.CompilerParams(
        dimension_semantics=("arbitrary",) * n_grid_dims,
        vmem_limit_bytes=VMEM_LIMIT_BYTES,
    )


def _tile(n, pref):
    t = min(pref, n)
    while n % t:
        t //= 2
    return t


def _silu(x):
    return x * (1.0 / (1.0 + jnp.exp(-x)))


def _ada_kernel(x_ref, w_ref, b_ref, o_ref):
    xa = _silu(x_ref[...]).astype(BF16)
    w = w_ref[...].astype(BF16)
    o_ref[...] = jnp.dot(xa, w, preferred_element_type=F32) + b_ref[...]


def _ada_params(cond, ada_w, ada_b):
    depth, d, n = ada_w.shape
    bn = _tile(n, 512)
    return pl.pallas_call(
        _ada_kernel,
        out_shape=jax.ShapeDtypeStruct((depth, ADA_ROWS, n), F32),
        grid=(depth, n // bn),
        in_specs=[
            pl.BlockSpec((ADA_ROWS, d), lambda l, j: (0, 0)),
            pl.BlockSpec((None, d, bn), lambda l, j: (l, 0, j)),
            pl.BlockSpec((None, 1, bn), lambda l, j: (l, 0, j)),
        ],
        out_specs=pl.BlockSpec((None, ADA_ROWS, bn), lambda l, j: (l, 0, j)),
        compiler_params=_params(2),
        name="ada_params",
    )(cond, ada_w, ada_b.reshape(depth, 1, n))


def _norm_mod_kernel(x_ref, nw_ref, sh_ref, sc_ref, *rest, n_experts):
    x = x_ref[...]
    y = x * lax.rsqrt(jnp.mean(x * x, axis=-1, keepdims=True) + EPS)
    u = (y * nw_ref[...]) * (1.0 + sc_ref[...]) + sh_ref[...]
    ub = u.astype(BF16)
    if n_experts == 0:
        (u_ref,) = rest
        u_ref[...] = ub
        return
    wr_ref, u_ref, comb_ref = rest
    u_ref[...] = ub
    logits = jnp.dot(ub, wr_ref[...], preferred_element_type=F32)
    lane = lax.broadcasted_iota(jnp.int32, logits.shape, 1).astype(F32)
    valid = lane < n_experts
    logits = jnp.where(valid, logits, NEG_INF)
    e = jnp.exp(logits - jnp.max(logits, axis=-1, keepdims=True))
    probs = jnp.where(valid, e / jnp.sum(e, axis=-1, keepdims=True), -1.0)
    p1 = jnp.max(probs, axis=-1, keepdims=True)
    i1 = jnp.min(jnp.where(probs == p1, lane, float(LANES)), axis=-1, keepdims=True)
    rest_p = jnp.where(lane == i1, -1.0, probs)
    p2 = jnp.max(rest_p, axis=-1, keepdims=True)
    i2 = jnp.min(jnp.where(rest_p == p2, lane, float(LANES)), axis=-1, keepdims=True)
    inv = 1.0 / (p1 + p2)
    comb_ref[...] = jnp.where(lane == i1, p1 * inv, jnp.where(lane == i2, p2 * inv, 0.0))


def _norm_mod(x, norm_w, shift, scale, rows_per_batch, router_w=None):
    m, d = x.shape
    bm = _tile(rows_per_batch, 512)
    tiles_per_batch = rows_per_batch // bm
    mod_spec = pl.BlockSpec((None, 1, d), lambda i: (i // tiles_per_batch, 0, 0))
    in_specs = [
        pl.BlockSpec((bm, d), lambda i: (i, 0)),
        pl.BlockSpec((1, d), lambda i: (0, 0)),
        mod_spec,
        mod_spec,
    ]
    args = [x, norm_w.reshape(1, d), shift, scale]
    u_shape = jax.ShapeDtypeStruct((m, d), BF16)
    u_spec = pl.BlockSpec((bm, d), lambda i: (i, 0))
    if router_w is None:
        n_experts = 0
        out_shape, out_specs = u_shape, u_spec
    else:
        n_experts = router_w.shape[1]
        wr = jnp.zeros((d, LANES), BF16).at[:, :n_experts].set(router_w.astype(BF16))
        in_specs.append(pl.BlockSpec((d, LANES), lambda i: (0, 0)))
        args.append(wr)
        out_shape = (u_shape, jax.ShapeDtypeStruct((m, LANES), F32))
        out_specs = (u_spec, pl.BlockSpec((bm, LANES), lambda i: (i, 0)))
    return pl.pallas_call(
        functools.partial(_norm_mod_kernel, n_experts=n_experts),
        out_shape=out_shape,
        grid=(m // bm,),
        in_specs=in_specs,
        out_specs=out_specs,
        compiler_params=_params(1),
        name="norm_mod",
    )(*args)


def _mm_na_qkv_kernel(x_ref, w_ref, qn_ref, kn_ref, o_ref, *, tiles_per_part, hd):
    j = pl.program_id(1)
    acc = jnp.dot(x_ref[...], w_ref[...], preferred_element_type=F32)
    n_heads = acc.shape[1] // hd

    def store_normed(nw):
        for t in range(n_heads):
            a = acc[:, t * hd:(t + 1) * hd]
            y = a * lax.rsqrt(jnp.mean(a * a, axis=-1, keepdims=True) + EPS)
            o_ref[:, t * hd:(t + 1) * hd] = (y * nw).astype(BF16)

    @pl.when(j < tiles_per_part)
    def _():
        store_normed(qn_ref[...])

    @pl.when((j >= tiles_per_part) & (j < 2 * tiles_per_part))
    def _():
        store_normed(kn_ref[...])

    @pl.when(j >= 2 * tiles_per_part)
    def _():
        o_ref[...] = acc.astype(BF16)


def _mm_na_qkv(x, w, layer, qn_w, kn_w):
    m, k = x.shape
    n = w.shape[2]
    hd = qn_w.shape[0]
    bm, bn = _tile(m, 1024), _tile(n // 3, 1024)
    return pl.pallas_call(
        functools.partial(_mm_na_qkv_kernel, tiles_per_part=n // 3 // bn, hd=hd),
        out_shape=jax.ShapeDtypeStruct((m, n), BF16),
        grid=(m // bm, n // bn),
        in_specs=[
            pl.BlockSpec((bm, k), lambda i, j: (i, 0)),
            pl.BlockSpec((None, k, bn), lambda i, j: (layer, 0, j)),
            pl.BlockSpec((1, hd), lambda i, j: (0, 0)),
            pl.BlockSpec((1, hd), lambda i, j: (0, 0)),
        ],
        out_specs=pl.BlockSpec((bm, bn), lambda i, j: (i, j)),
        compiler_params=_params(2),
        name="mm_na_qkv",
    )(x, w, qn_w.reshape(1, hd), kn_w.reshape(1, hd))


def _mm_ret_qkvg_kernel(x_ref, w_ref, *rest, tiles_per_part, hd, rope):
    if rope:
        cos_ref, sin_ref, o_ref = rest
    else:
        (o_ref,) = rest
    j = pl.program_id(1)
    acc = jnp.dot(x_ref[...], w_ref[...], preferred_element_type=F32)
    n_heads = acc.shape[1] // hd
    k_scale = hd ** -0.5

    def store_qk(scale):
        for t in range(n_heads):
            a = acc[:, t * hd:(t + 1) * hd]
            if rope:
                lane = lax.broadcasted_iota(jnp.int32, a.shape, 1)
                first = (lane % (hd // 2)) < (hd // 4)
                partner = jnp.where(first, pltpu.roll(a, hd - hd // 4, 1), pltpu.roll(a, hd // 4, 1))
                a = a * cos_ref[...] + partner * sin_ref[...]
            o_ref[:, t * hd:(t + 1) * hd] = (a * scale).astype(BF16)

    @pl.when(j < tiles_per_part)
    def _():
        store_qk(1.0)

    @pl.when((j >= tiles_per_part) & (j < 2 * tiles_per_part))
    def _():
        store_qk(k_scale)

    @pl.when((j >= 2 * tiles_per_part) & (j < 3 * tiles_per_part))
    def _():
        o_ref[...] = acc.astype(BF16)

    @pl.when(j >= 3 * tiles_per_part)
    def _():
        o_ref[...] = _silu(acc).astype(BF16)


def _mm_ret_qkvg(x, w, layer, hd, rope_tables, seq):
    m, k = x.shape
    n = w.shape[2]
    bm, bn = _tile(seq, 1024), _tile(n // 4, 1024)
    rope = rope_tables is not None
    in_specs = [
        pl.BlockSpec((bm, k), lambda i, j: (i, 0)),
        pl.BlockSpec((None, k, bn), lambda i, j: (layer, 0, j)),
    ]
    args = [x, w]
    if rope:
        tiles_per_seq = seq // bm
        tab_spec = pl.BlockSpec((bm, hd), lambda i, j: (i % tiles_per_seq, 0))
        in_specs += [tab_spec, tab_spec]
        args += list(rope_tables)
    return pl.pallas_call(
        functools.partial(_mm_ret_qkvg_kernel, tiles_per_part=n // 4 // bn, hd=hd, rope=rope),
        out_shape=jax.ShapeDtypeStruct((m, n), BF16),
        grid=(m // bm, n // bn),
        in_specs=in_specs,
        out_specs=pl.BlockSpec((bm, bn), lambda i, j: (i, j)),
        compiler_params=_params(2),
        name="mm_ret_qkvg",
    )(*args)


def _mm_swiglu_kernel(x_ref, wa_ref, wb_ref, *rest, tiles_per_expert):
    x = x_ref[...]
    a = jnp.dot(x, wa_ref[...], preferred_element_type=F32)
    b = jnp.dot(x, wb_ref[...], preferred_element_type=F32)
    act = _silu(a) * b
    if tiles_per_expert:
        comb_ref, o_ref = rest
        e = pl.program_id(1) // tiles_per_expert
        comb = comb_ref[...]
        lane = lax.broadcasted_iota(jnp.int32, comb.shape, 1)
        act = act * jnp.sum(jnp.where(lane == e, comb, 0.0), axis=-1, keepdims=True)
    else:
        (o_ref,) = rest
    o_ref[...] = act.astype(BF16)


def _mm_swiglu(x, w13, layer, comb=None):
    m, k = x.shape
    bm = _tile(m, 1024)
    if comb is None:
        f = w13.shape[2] // 2
        bn = _tile(f, 512)
        nt = f // bn
        wa_spec = pl.BlockSpec((None, k, bn), lambda i, j: (layer, 0, j))
        wb_spec = pl.BlockSpec((None, k, bn), lambda i, j: (layer, 0, j + nt))
        in_specs = [pl.BlockSpec((bm, k), lambda i, j: (i, 0)), wa_spec, wb_spec]
        args = [x, w13, w13]
        tpe = 0
        n_out = f
    else:
        _, n_e, _, ed2 = w13.shape
        ed = ed2 // 2
        bn = _tile(ed, 512)
        tpe = ed // bn
        wa_spec = pl.BlockSpec((None, None, k, bn), lambda i, j: (layer, j // tpe, 0, j % tpe))
        wb_spec = pl.BlockSpec((None, None, k, bn), lambda i, j: (layer, j // tpe, 0, tpe + j % tpe))
        in_specs = [pl.BlockSpec((bm, k), lambda i, j: (i, 0)), wa_spec, wb_spec,
                    pl.BlockSpec((bm, LANES), lambda i, j: (i, 0))]
        args = [x, w13, w13, comb]
        n_out = n_e * ed
    return pl.pallas_call(
        functools.partial(_mm_swiglu_kernel, tiles_per_expert=tpe),
        out_shape=jax.ShapeDtypeStruct((m, n_out), BF16),
        grid=(m // bm, n_out // bn),
        in_specs=in_specs,
        out_specs=pl.BlockSpec((bm, bn), lambda i, j: (i, j)),
        compiler_params=_params(2),
        name="mm_swiglu",
    )(*args)


def _mm_res_kernel(x_ref, w_ref, res_ref, gate_ref, o_ref):
    acc = jnp.dot(x_ref[...], w_ref[...], preferred_element_type=F32)
    o_ref[...] = res_ref[...] + gate_ref[...] * acc


def _mm_res(x, w, layer, res, gate, rows_per_batch):
    m, k = x.shape
    n = w.shape[2]
    bm, bn = _tile(rows_per_batch, 1024), _tile(n, 512)
    tiles_per_batch = rows_per_batch // bm
    return pl.pallas_call(
        _mm_res_kernel,
        out_shape=jax.ShapeDtypeStruct((m, n), F32),
        grid=(m // bm, n // bn),
        in_specs=[
            pl.BlockSpec((bm, k), lambda i, j: (i, 0)),
            pl.BlockSpec((None, k, bn), lambda i, j: (layer, 0, j)),
            pl.BlockSpec((bm, bn), lambda i, j: (i, j)),
            pl.BlockSpec((None, 1, bn), lambda i, j: (i // tiles_per_batch, 0, j)),
        ],
        out_specs=pl.BlockSpec((bm, bn), lambda i, j: (i, j)),
        compiler_params=_params(2),
        name="mm_res",
    )(x, w, res, gate)


LOG2E = 1.4426950408889634


def _softmax_pv(scores, values):
    m = functools.reduce(jnp.maximum, [jnp.max(s, axis=-1, keepdims=True) for s in scores])
    ps = [jnp.exp2(s - m) for s in scores]
    denom = functools.reduce(jnp.add, [jnp.sum(p, axis=-1, keepdims=True) for p in ps])
    out = functools.reduce(
        jnp.add,
        [jnp.dot(p.astype(BF16), v, preferred_element_type=F32) for p, v in zip(ps, values)])
    return out / denom


def _qk(q, k):
    return lax.dot_general(q, k, (((1,), (1,)), ((), ())), preferred_element_type=F32)


NA_Q_ROWS = 4
NA_KEY_ROWS = 12


def _na_block_geometry(t, rows, win_r):
    u0 = min(max(t * NA_Q_ROWS - win_r // 2, 0), rows - NA_KEY_ROWS)
    geo = []
    for a in range(NA_Q_ROWS):
        r = t * NA_Q_ROWS + a
        r0 = min(max(r - win_r // 2, 0), rows - win_r)
        geo.append([(r0 <= u0 + j < r0 + win_r, u0 + j - r + win_r - 1) for j in range(NA_KEY_ROWS)])
    return u0, geo


def _na_build_slabs(pairs_ref, slab_ref, rows, win_r):
    n_blk = rows // NA_Q_ROWS
    lane = lax.broadcasted_iota(jnp.int32, (GRID_W, 2 * GRID_W), 1)
    neg = jnp.full((GRID_W, 2 * GRID_W), NEG_INF, F32)
    for v, t in enumerate((0, 1, n_blk - 1)):
        _, geo = _na_block_geometry(t, rows, win_r)
        for a in range(NA_Q_ROWS):
            for jj in range(NA_KEY_ROWS // 2):
                (ok0, d0), (ok1, _) = geo[a][2 * jj], geo[a][2 * jj + 1]
                if ok0 and ok1:
                    tile = pairs_ref[d0 + 1]
                elif ok0:
                    tile = jnp.where(lane < GRID_W, pairs_ref[d0 + 1], neg)
                elif ok1:
                    tile = jnp.where(lane >= GRID_W, pairs_ref[d0 + 1], neg)
                else:
                    tile = neg
                slab_ref[v, a * GRID_W:(a + 1) * GRID_W, jj * 2 * GRID_W:(jj + 1) * 2 * GRID_W] = tile


def _na_attn_kernel(q_ref, k_ref, v_ref, qc_ref, kc_ref, vc_ref, pairs_ref, o_ref, oc_ref, bias_ref,
                    *, rows, win_r, scale):
    @pl.when(pl.program_id(1) == 0)
    def _():
        _na_build_slabs(pairs_ref, bias_ref, rows, win_r)

    kc = kc_ref[...]
    vc = vc_ref[...]
    n_blk = rows // NA_Q_ROWS
    n_q = NA_Q_ROWS * GRID_W
    n_k = NA_KEY_ROWS * GRID_W

    def query_block(t, carry):
        u0 = jnp.clip(t * NA_Q_ROWS - win_r // 2, 0, rows - NA_KEY_ROWS)
        slab = jnp.where(t == 0, 0, jnp.where(t == n_blk - 1, 2, 1))
        q_rows = pl.ds(pl.multiple_of(t * n_q, n_q), n_q)
        k_rows = pl.ds(pl.multiple_of(u0 * GRID_W, GRID_W), n_k)
        q = q_ref[q_rows, :]
        s_win = _qk(q, k_ref[k_rows, :]) * scale + bias_ref[slab]
        s_ctx = _qk(q, kc) * scale
        o = _softmax_pv([s_win, s_ctx], [v_ref[k_rows, :], vc])
        o_ref[q_rows, :] = o.astype(BF16)
        return carry

    lax.fori_loop(0, n_blk, query_block, 0, unroll=4)
    oc_ref[...] = _softmax_pv([_qk(qc_ref[...], kc) * scale], [vc]).astype(BF16)


def _na_bias_pairs(rpb):
    n_heads, n_dr, n_dc = rpb.shape
    win_c = (n_dc + 1) // 2
    cols = jnp.arange(GRID_W)
    c0 = jnp.clip(cols - win_c // 2, 0, GRID_W - win_c)
    col_in = (cols[None, :] >= c0[:, None]) & (cols[None, :] < c0[:, None] + win_c)
    dc = jnp.clip(cols[None, :] - cols[:, None] + win_c - 1, 0, 2 * win_c - 2)
    masked = jnp.where(col_in, rpb[:, :, dc].astype(F32) * LOG2E, NEG_INF)
    neg = jnp.full((n_heads, 1, GRID_W, GRID_W), NEG_INF, F32)
    ext = jnp.concatenate([neg, masked, neg], axis=1)
    return jnp.concatenate([ext[:, :-1], ext[:, 1:]], axis=-1)


def _na_attention(qkv, qkvc, rpb, batch, seq, ctx_len, hd):
    n_heads = rpb.shape[0]
    win_r = (rpb.shape[1] + 1) // 2
    rows = seq // GRID_W
    n_blk = rows // NA_Q_ROWS
    assert rows % NA_Q_ROWS == 0 and n_blk >= 3 and NA_Q_ROWS >= win_r // 2
    assert (n_blk - 2) * NA_Q_ROWS - win_r // 2 <= rows - NA_KEY_ROWS
    assert NA_KEY_ROWS >= win_r + NA_Q_ROWS - 1 and NA_KEY_ROWS % 2 == 0
    d = n_heads * hd
    pairs = _na_bias_pairs(rpb)

    def spec(n_rows, part):
        return pl.BlockSpec((n_rows, hd), lambda h, b: (b, part * n_heads + h))

    return pl.pallas_call(
        functools.partial(_na_attn_kernel, rows=rows, win_r=win_r, scale=hd ** -0.5 * LOG2E),
        out_shape=(jax.ShapeDtypeStruct((batch * seq, d), BF16),
                   jax.ShapeDtypeStruct((batch * ctx_len, d), BF16)),
        grid=(n_heads, batch),
        in_specs=[spec(seq, 0), spec(seq, 1), spec(seq, 2),
                  spec(ctx_len, 0), spec(ctx_len, 1), spec(ctx_len, 2),
                  pl.BlockSpec((None,) + pairs.shape[1:], lambda h, b: (h, 0, 0, 0))],
        out_specs=(pl.BlockSpec((seq, hd), lambda h, b: (b, h)),
                   pl.BlockSpec((ctx_len, hd), lambda h, b: (b, h))),
        scratch_shapes=[pltpu.VMEM((3, NA_Q_ROWS * GRID_W, NA_KEY_ROWS * GRID_W), F32)],
        compiler_params=_params(2),
        name="na_attention",
    )(qkv, qkv, qkv, qkvc, qkvc, qkvc, pairs)


def _kv_outer(k, v):
    return lax.dot_general(k, v, (((0,), (0,)), ((), ())), preferred_element_type=F32)


def _retention_kernel(lgf_ref, lgb_ref, q_ref, k_ref, v_ref, g_ref, gnw_ref, *rest,
                      n_chunks, ctx_len):
    if ctx_len:
        kc_ref, vc_ref, o_ref, acc_ref, sf_ref, sb_ref = rest
    else:
        o_ref, acc_ref, sf_ref, sb_ref = rest
    c_len = RET_CHUNK
    h = pl.program_id(0)
    lgf = jnp.full((1, 1), lgf_ref[h], F32)
    lgb = jnp.full((1, 1), lgb_ref[h], F32)

    row = lax.broadcasted_iota(jnp.int32, (c_len, c_len), 0)
    col = lax.broadcasted_iota(jnp.int32, (c_len, c_len), 1)
    diff = (row - col).astype(F32)
    intra = jnp.where(diff >= 0, jnp.exp(lgf * jnp.maximum(diff, 0.0)),
                      jnp.exp(lgb * jnp.maximum(-diff, 0.0)))
    pos = lax.broadcasted_iota(jnp.int32, (c_len, 1), 0).astype(F32)
    q_dec_f = jnp.exp(lgf * (pos + 1.0))
    k_dec_f = jnp.exp(lgf * (c_len - 1.0 - pos))
    chunk_dec_f = jnp.exp(lgf * c_len)
    q_dec_b = jnp.exp(lgb * (c_len - pos))
    k_dec_b = jnp.exp(lgb * pos)
    chunk_dec_b = jnp.exp(lgb * c_len)

    if ctx_len:
        cpos = lax.broadcasted_iota(jnp.int32, (ctx_len, 1), 0).astype(F32)
        kc = kc_ref[...].astype(F32)
        vc = vc_ref[...]
        sf_ref[...] = _kv_outer((kc * jnp.exp(lgf * (ctx_len - 1.0 - cpos))).astype(BF16), vc)
        sb_ref[...] = _kv_outer((kc * jnp.exp(lgb * cpos)).astype(BF16), vc)
    else:
        sf_ref[...] = jnp.zeros_like(sf_ref)
        sb_ref[...] = jnp.zeros_like(sb_ref)

    def chunk_slice(c):
        return pl.ds(pl.multiple_of(c * c_len, c_len), c_len)

    def fwd_part(sl):
        q = q_ref[sl, :]
        k = k_ref[sl, :]
        v = v_ref[sl, :]
        s = (_qk(q, k) * intra).astype(BF16)
        qf = (q.astype(F32) * q_dec_f).astype(BF16)
        val = (jnp.dot(s, v, preferred_element_type=F32)
               + jnp.dot(qf, sf_ref[...].astype(BF16), preferred_element_type=F32))
        kf = (k.astype(F32) * k_dec_f).astype(BF16)
        sf_ref[...] = sf_ref[...] * chunk_dec_f + _kv_outer(kf, v)
        return val

    def bwd_part(sl):
        q = q_ref[sl, :]
        k = k_ref[sl, :]
        v = v_ref[sl, :]
        qb = (q.astype(F32) * q_dec_b).astype(BF16)
        val = jnp.dot(qb, sb_ref[...].astype(BF16), preferred_element_type=F32)
        kb = (k.astype(F32) * k_dec_b).astype(BF16)
        sb_ref[...] = sb_ref[...] * chunk_dec_b + _kv_outer(kb, v)
        return val

    def finish(sl, o):
        mu = jnp.mean(o, axis=-1, keepdims=True)
        cen = o - mu
        var = jnp.mean(cen * cen, axis=-1, keepdims=True)
        normed = cen * lax.rsqrt(var + EPS)
        o_ref[sl, :] = (normed * gnw_ref[...] * g_ref[sl, :].astype(F32)).astype(BF16)

    half = n_chunks // 2

    def first_half(t, carry):
        lo, hi = chunk_slice(t), chunk_slice(n_chunks - 1 - t)
        acc_ref[lo, :] = fwd_part(lo)
        acc_ref[hi, :] = bwd_part(hi)
        return carry

    def second_half(t, carry):
        hi, lo = chunk_slice(half + t), chunk_slice(half - 1 - t)
        finish(hi, acc_ref[hi, :] + fwd_part(hi))
        finish(lo, acc_ref[lo, :] + bwd_part(lo))
        return carry

    unroll = 2 if half % 2 == 0 else 1
    lax.fori_loop(0, half, first_half, 0, unroll=unroll)
    lax.fori_loop(0, half, second_half, 0, unroll=unroll)


def _retention(qkvg, lg_f, lg_b, gn_w, batch, n_tok, hd, qkvg_ctx=None, ctx_len=0):
    n_heads = lg_f.shape[0]
    d = n_heads * hd
    n_chunks = n_tok // RET_CHUNK
    assert n_tok % (2 * RET_CHUNK) == 0

    def spec(n_rows, part):
        return pl.BlockSpec((n_rows, hd), lambda h, b: (b, part * n_heads + h))

    smem = pl.BlockSpec(memory_space=pltpu.SMEM)
    in_specs = [smem, smem, spec(n_tok, 0), spec(n_tok, 1), spec(n_tok, 2), spec(n_tok, 3),
                pl.BlockSpec((1, hd), lambda h, b: (0, h))]
    args = [lg_f, lg_b, qkvg, qkvg, qkvg, qkvg, gn_w.reshape(1, d)]
    if ctx_len:
        in_specs += [spec(ctx_len, 1), spec(ctx_len, 2)]
        args += [qkvg_ctx, qkvg_ctx]
    return pl.pallas_call(
        functools.partial(_retention_kernel, n_chunks=n_chunks, ctx_len=ctx_len),
        out_shape=jax.ShapeDtypeStruct((batch * n_tok, d), BF16),
        grid=(n_heads, batch),
        in_specs=in_specs,
        out_specs=pl.BlockSpec((n_tok, hd), lambda h, b: (b, h)),
        scratch_shapes=[pltpu.VMEM((n_tok, hd), F32),
                        pltpu.VMEM((hd, hd), F32),
                        pltpu.VMEM((hd, hd), F32)],
        compiler_params=_params(2),
        name="retention",
    )(*args)


def _rope_tables(seq, hd):
    t = jnp.arange(seq)
    row = (t // GRID_W).astype(F32)
    col = (t % GRID_W).astype(F32)
    axis_dim = hd // 2
    inv_freq = jnp.power(ROPE_BASE, -jnp.arange(0, axis_dim, 2, dtype=F32) / axis_dim)
    ang_r = row[:, None] * inv_freq[None, :]
    ang_c = col[:, None] * inv_freq[None, :]
    cos = jnp.concatenate([jnp.cos(ang_r), jnp.cos(ang_r), jnp.cos(ang_c), jnp.cos(ang_c)], axis=-1)
    sin = jnp.concatenate([-jnp.sin(ang_r), jnp.sin(ang_r), -jnp.sin(ang_c), jnp.sin(ang_c)], axis=-1)
    return cos, sin


def kernel(x, c, ctx, c_ctx, ada_w, ada_b, norm1_w, norm2_w, na_wqkv, na_wo, na_qnorm_w, na_knorm_w, na_rpb, ret_wqkvg, ret_wo, ret_decay_f, ret_decay_b, ret_gn_w, ffn_w13, ffn_w2, moe_router, moe_w13, moe_w2):
    batch, seq, d = x.shape
    ctx_len = ctx.shape[1]
    depth = ada_w.shape[0]
    na_hd = na_qnorm_w.shape[1]
    ret_heads = ret_decay_f.shape[1]
    ret_hd = d // ret_heads
    assert batch + 1 <= ADA_ROWS

    cond = jnp.zeros((ADA_ROWS, d), F32).at[:batch].set(c).at[batch].set(c_ctx)
    ada = _ada_params(cond, ada_w, ada_b)
    rope = _rope_tables(seq, ret_hd)

    wb = {
        "na_wqkv": na_wqkv.astype(BF16), "na_wo": na_wo.astype(BF16),
        "ret_wqkvg": ret_wqkvg.astype(BF16), "ret_wo": ret_wo.astype(BF16),
        "ffn_w13": ffn_w13.astype(BF16), "ffn_w2": ffn_w2.astype(BF16),
        "moe_w13": moe_w13.astype(BF16),
        "moe_w2": moe_w2.astype(BF16).reshape(moe_w2.shape[0], -1, d),
    }

    h = x.reshape(batch * seq, d)
    hc = ctx.reshape(batch * ctx_len, d)
    for i in range(depth):
        j = i // 2
        ctx_out = i < depth - 1
        mods = ada[i].reshape(ADA_ROWS, 6, 1, d)
        sh1, sc1, g1, sh2, sc2, g2 = (mods[:batch, p] for p in range(6))
        csh1, csc1, cg1, csh2, csc2, cg2 = (mods[batch:batch + 1, p] for p in range(6))

        u = _norm_mod(h, norm1_w[i], sh1, sc1, seq)
        uc = _norm_mod(hc, norm1_w[i], csh1, csc1, batch * ctx_len)
        if i % 2 == 0:
            w_o = wb["na_wo"]
            qkv = _mm_na_qkv(u, wb["na_wqkv"], j, na_qnorm_w[j], na_knorm_w[j])
            qkvc = _mm_na_qkv(uc, wb["na_wqkv"], j, na_qnorm_w[j], na_knorm_w[j])
            o, oc = _na_attention(qkv, qkvc, na_rpb[j], batch, seq, ctx_len, na_hd)
        else:
            w_o = wb["ret_wo"]
            lg_f = jax.nn.log_sigmoid(ret_decay_f[j].astype(F32))
            lg_b = jax.nn.log_sigmoid(ret_decay_b[j].astype(F32))
            qkvg = _mm_ret_qkvg(u, wb["ret_wqkvg"], j, ret_hd, rope, seq)
            qkvgc = _mm_ret_qkvg(uc, wb["ret_wqkvg"], j, ret_hd, None, batch * ctx_len)
            o = _retention(qkvg, lg_f, lg_b, ret_gn_w[j], batch, seq, ret_hd, qkvgc, ctx_len)
            if ctx_out:
                oc = _retention(qkvgc, lg_f, lg_b, ret_gn_w[j], batch, ctx_len, ret_hd)
        h = _mm_res(o, w_o, j, h, g1, seq)
        if ctx_out:
            hc = _mm_res(oc, w_o, j, hc, cg1, batch * ctx_len)

        if i % 2 == 0:
            w13, w2 = wb["ffn_w13"], wb["ffn_w2"]
            u2 = _norm_mod(h, norm2_w[i], sh2, sc2, seq)
            h = _mm_res(_mm_swiglu(u2, w13, j), w2, j, h, g2, seq)
            if ctx_out:
                uc2 = _norm_mod(hc, norm2_w[i], csh2, csc2, batch * ctx_len)
                hc = _mm_res(_mm_swiglu(uc2, w13, j), w2, j, hc, cg2, batch * ctx_len)
        else:
            w13, w2 = wb["moe_w13"], wb["moe_w2"]
            u2, comb = _norm_mod(h, norm2_w[i], sh2, sc2, seq, moe_router[j])
            h = _mm_res(_mm_swiglu(u2, w13, j, comb), w2, j, h, g2, seq)
            if ctx_out:
                uc2, combc = _norm_mod(hc, norm2_w[i], csh2, csc2, batch * ctx_len, moe_router[j])
                hc = _mm_res(_mm_swiglu(uc2, w13, j, combc), w2, j, hc, cg2, batch * ctx_len)
    return h.reshape(batch, seq, d)
```

```python
import functools

import jax
import jax.numpy as jnp
from jax import lax
from jax.experimental import pallas as pl
from jax.experimental.pallas import tpu as pltpu

F32 = jnp.float32
BF16 = jnp.bfloat16

GRID_W = 64
RET_CHUNK = 128
ROPE_BASE = 10000.0
EPS = 1e-6
NEG_INF = -1e30
LOG2E = 1.4426950408889634

V7X_VMEM_BYTES = 64 * 1024 * 1024
VMEM_LIMIT_BYTES = V7X_VMEM_BYTES * 7 // 8
LANES = 128
ADA_ROWS = 8


def _params(n_grid_dims):
    return pltpu.CompilerParams(
        dimension_semantics=("arbitrary",) * n_grid_dims,
        vmem_limit_bytes=VMEM_LIMIT_BYTES,
    )


def _tile(n, pref):
    t = min(pref, n)
    while n % t:
        t //= 2
    return t


def _silu(x):
    return x * (1.0 / (1.0 + jnp.exp(-x)))


def _ada_kernel(x_ref, w_ref, b_ref, o_ref):
    xa = _silu(x_ref[...]).astype(BF16)
    w = w_ref[...].astype(BF16)
    o_ref[...] = jnp.dot(xa, w, preferred_element_type=F32) + b_ref[...]


def _ada_params(cond, ada_w, ada_b):
    depth, d, n = ada_w.shape
    bn = _tile(n, 512)
    return pl.pallas_call(
        _ada_kernel,
        out_shape=jax.ShapeDtypeStruct((depth, ADA_ROWS, n), F32),
        grid=(depth, n // bn),
        in_specs=[
            pl.BlockSpec((ADA_ROWS, d), lambda l, j: (0, 0)),
            pl.BlockSpec((None, d, bn), lambda l, j: (l, 0, j)),
            pl.BlockSpec((None, 1, bn), lambda l, j: (l, 0, j)),
        ],
        out_specs=pl.BlockSpec((None, ADA_ROWS, bn), lambda l, j: (l, 0, j)),
        compiler_params=_params(2),
        name="ada_params",
    )(cond, ada_w, ada_b.reshape(depth, 1, n))


def _norm_mod_kernel(x_ref, nw_ref, sh_ref, sc_ref, *rest, n_experts):
    x = x_ref[...]
    y = x * lax.rsqrt(jnp.mean(x * x, axis=-1, keepdims=True) + EPS)
    u = (y * nw_ref[...]) * (1.0 + sc_ref[...]) + sh_ref[...]
    ub = u.astype(BF16)
    if n_experts == 0:
        (u_ref,) = rest
        u_ref[...] = ub
        return
    wr_ref, u_ref, comb_ref = rest
    u_ref[...] = ub
    logits = jnp.dot(ub, wr_ref[...], preferred_element_type=F32)
    lane = lax.broadcasted_iota(jnp.int32, logits.shape, 1).astype(F32)
    valid = lane < n_experts
    logits = jnp.where(valid, logits, NEG_INF)
    e = jnp.exp(logits - jnp.max(logits, axis=-1, keepdims=True))
    probs = jnp.where(valid, e / jnp.sum(e, axis=-1, keepdims=True), -1.0)
    p1 = jnp.max(probs, axis=-1, keepdims=True)
    i1 = jnp.min(jnp.where(probs == p1, lane, float(LANES)), axis=-1, keepdims=True)
    rest_p = jnp.where(lane == i1, -1.0, probs)
    p2 = jnp.max(rest_p, axis=-1, keepdims=True)
    i2 = jnp.min(jnp.where(rest_p == p2, lane, float(LANES)), axis=-1, keepdims=True)
    inv = 1.0 / (p1 + p2)
    comb_ref[...] = jnp.where(lane == i1, p1 * inv, jnp.where(lane == i2, p2 * inv, 0.0))


def _norm_mod(x, norm_w, shift, scale, rows_per_batch, router_w=None):
    m, d = x.shape
    bm = _tile(rows_per_batch, 512)
    tiles_per_batch = rows_per_batch // bm
    mod_spec = pl.BlockSpec((None, 1, d), lambda i: (i // tiles_per_batch, 0, 0))
    in_specs = [
        pl.BlockSpec((bm, d), lambda i: (i, 0)),
        pl.BlockSpec((1, d), lambda i: (0, 0)),
        mod_spec,
        mod_spec,
    ]
    args = [x, norm_w.reshape(1, d), shift, scale]
    u_shape = jax.ShapeDtypeStruct((m, d), BF16)
    u_spec = pl.BlockSpec((bm, d), lambda i: (i, 0))
    if router_w is None:
        n_experts = 0
        out_shape, out_specs = u_shape, u_spec
    else:
        n_experts = router_w.shape[1]
        wr = jnp.zeros((d, LANES), BF16).at[:, :n_experts].set(router_w.astype(BF16))
        in_specs.append(pl.BlockSpec((d, LANES), lambda i: (0, 0)))
        args.append(wr)
        out_shape = (u_shape, jax.ShapeDtypeStruct((m, LANES), F32))
        out_specs = (u_spec, pl.BlockSpec((bm, LANES), lambda i: (i, 0)))
    return pl.pallas_call(
        functools.partial(_norm_mod_kernel, n_experts=n_experts),
        out_shape=out_shape,
        grid=(m // bm,),
        in_specs=in_specs,
        out_specs=out_specs,
        compiler_params=_params(1),
        name="norm_mod",
    )(*args)


def _weight_tile(w_ref, wout_ref):
    if wout_ref is None:
        return w_ref[...]
    w = w_ref[...].astype(BF16)
    wout_ref[...] = w
    return w


def _mm_na_qkv_kernel(x_ref, w_ref, qn_ref, kn_ref, o_ref, *wout, tiles_per_part, hd):
    j = pl.program_id(1)
    w = _weight_tile(w_ref, wout[0] if wout else None)
    acc = jnp.dot(x_ref[...], w, preferred_element_type=F32)
    n_heads = acc.shape[1] // hd

    def store_normed(nw):
        for t in range(n_heads):
            a = acc[:, t * hd:(t + 1) * hd]
            y = a * lax.rsqrt(jnp.mean(a * a, axis=-1, keepdims=True) + EPS)
            o_ref[:, t * hd:(t + 1) * hd] = (y * nw).astype(BF16)

    @pl.when(j < tiles_per_part)
    def _():
        store_normed(qn_ref[...])

    @pl.when((j >= tiles_per_part) & (j < 2 * tiles_per_part))
    def _():
        store_normed(kn_ref[...])

    @pl.when(j >= 2 * tiles_per_part)
    def _():
        o_ref[...] = acc.astype(BF16)


def _cast_outputs(cast, out_shape, out_spec, w_shapes, w_specs):
    if not cast:
        return out_shape, out_spec
    return ((out_shape,) + tuple(jax.ShapeDtypeStruct(s, BF16) for s in w_shapes),
            (out_spec,) + tuple(w_specs))


def _mm_na_qkv(x, w, layer, qn_w, kn_w, cast=False):
    m, k = x.shape
    n = w.shape[2]
    hd = qn_w.shape[0]
    bm, bn = _tile(m, 1024), _tile(n // 3, 512 if cast else 1024)
    assert not cast or m == bm
    out_shape, out_specs = _cast_outputs(
        cast, jax.ShapeDtypeStruct((m, n), BF16), pl.BlockSpec((bm, bn), lambda i, j: (i, j)),
        [(1, k, n)], [pl.BlockSpec((None, k, bn), lambda i, j: (0, 0, j))])
    return pl.pallas_call(
        functools.partial(_mm_na_qkv_kernel, tiles_per_part=n // 3 // bn, hd=hd),
        out_shape=out_shape,
        grid=(m // bm, n // bn),
        in_specs=[
            pl.BlockSpec((bm, k), lambda i, j: (i, 0)),
            pl.BlockSpec((None, k, bn), lambda i, j: (layer, 0, j)),
            pl.BlockSpec((1, hd), lambda i, j: (0, 0)),
            pl.BlockSpec((1, hd), lambda i, j: (0, 0)),
        ],
        out_specs=out_specs,
        compiler_params=_params(2),
        name="mm_na_qkv",
    )(x, w, qn_w.reshape(1, hd), kn_w.reshape(1, hd))


def _mm_ret_qkvg_kernel(x_ref, w_ref, *rest, tiles_per_part, hd, rope, cast):
    if rope:
        cos_ref, sin_ref = rest[:2]
        rest = rest[2:]
    o_ref = rest[0]
    j = pl.program_id(1)
    w = _weight_tile(w_ref, rest[1] if cast else None)
    acc = jnp.dot(x_ref[...], w, preferred_element_type=F32)
    n_heads = acc.shape[1] // hd
    k_scale = hd ** -0.5

    def store_qk(scale):
        for t in range(n_heads):
            a = acc[:, t * hd:(t + 1) * hd]
            if rope:
                lane = lax.broadcasted_iota(jnp.int32, a.shape, 1)
                first = (lane % (hd // 2)) < (hd // 4)
                partner = jnp.where(first, pltpu.roll(a, hd - hd // 4, 1), pltpu.roll(a, hd // 4, 1))
                a = a * cos_ref[...] + partner * sin_ref[...]
            o_ref[:, t * hd:(t + 1) * hd] = (a * scale).astype(BF16)

    @pl.when(j < tiles_per_part)
    def _():
        store_qk(1.0)

    @pl.when((j >= tiles_per_part) & (j < 2 * tiles_per_part))
    def _():
        store_qk(k_scale)

    @pl.when((j >= 2 * tiles_per_part) & (j < 3 * tiles_per_part))
    def _():
        o_ref[...] = acc.astype(BF16)

    @pl.when(j >= 3 * tiles_per_part)
    def _():
        o_ref[...] = _silu(acc).astype(BF16)


def _mm_ret_qkvg(x, w, layer, hd, rope_tables, seq, cast=False):
    m, k = x.shape
    n = w.shape[2]
    bm, bn = _tile(seq, 1024), _tile(n // 4, 512 if cast else 1024)
    assert not cast or m == bm
    rope = rope_tables is not None
    in_specs = [
        pl.BlockSpec((bm, k), lambda i, j: (i, 0)),
        pl.BlockSpec((None, k, bn), lambda i, j: (layer, 0, j)),
    ]
    args = [x, w]
    if rope:
        tiles_per_seq = seq // bm
        tab_spec = pl.BlockSpec((bm, hd), lambda i, j: (i % tiles_per_seq, 0))
        in_specs += [tab_spec, tab_spec]
        args += list(rope_tables)
    out_shape, out_specs = _cast_outputs(
        cast, jax.ShapeDtypeStruct((m, n), BF16), pl.BlockSpec((bm, bn), lambda i, j: (i, j)),
        [(1, k, n)], [pl.BlockSpec((None, k, bn), lambda i, j: (0, 0, j))])
    return pl.pallas_call(
        functools.partial(_mm_ret_qkvg_kernel, tiles_per_part=n // 4 // bn, hd=hd, rope=rope, cast=cast),
        out_shape=out_shape,
        grid=(m // bm, n // bn),
        in_specs=in_specs,
        out_specs=out_specs,
        compiler_params=_params(2),
        name="mm_ret_qkvg",
    )(*args)


def _mm_swiglu_kernel(x_ref, wa_ref, wb_ref, *rest, tiles_per_expert, cast):
    if tiles_per_expert:
        comb_ref = rest[0]
        rest = rest[1:]
    o_ref = rest[0]
    x = x_ref[...]
    a = jnp.dot(x, _weight_tile(wa_ref, rest[1] if cast else None), preferred_element_type=F32)
    b = jnp.dot(x, _weight_tile(wb_ref, rest[2] if cast else None), preferred_element_type=F32)
    act = _silu(a) * b
    if tiles_per_expert:
        e = pl.program_id(1) // tiles_per_expert
        comb = comb_ref[...]
        lane = lax.broadcasted_iota(jnp.int32, comb.shape, 1)
        act = act * jnp.sum(jnp.where(lane == e, comb, 0.0), axis=-1, keepdims=True)
    o_ref[...] = act.astype(BF16)


def _mm_swiglu(x, wa, wb, layer, comb=None, cast=False):
    m, k = x.shape
    bm = _tile(m, 1024)
    assert not cast or m == bm
    moe = comb is not None
    f = wa.shape[-1] // 2 if cast else wa.shape[-1]
    bn = _tile(f, 256 if cast else 512)
    nt = f // bn
    b_off = nt if cast else 0
    if moe:
        n_e = wa.shape[1]
        wa_spec = pl.BlockSpec((None, None, k, bn), lambda i, j: (layer, j // nt, 0, j % nt))
        wb_spec = pl.BlockSpec((None, None, k, bn), lambda i, j: (layer, j // nt, 0, b_off + j % nt))
        w_shape = (1, n_e, k, f)
        wo_spec = pl.BlockSpec((None, None, k, bn), lambda i, j: (0, j // nt, 0, j % nt))
        n_out = n_e * f
    else:
        wa_spec = pl.BlockSpec((None, k, bn), lambda i, j: (layer, 0, j))
        wb_spec = pl.BlockSpec((None, k, bn), lambda i, j: (layer, 0, b_off + j))
        w_shape = (1, k, f)
        wo_spec = pl.BlockSpec((None, k, bn), lambda i, j: (0, 0, j))
        n_out = f
    in_specs = [pl.BlockSpec((bm, k), lambda i, j: (i, 0)), wa_spec, wb_spec]
    args = [x, wa, wb]
    if moe:
        in_specs.append(pl.BlockSpec((bm, LANES), lambda i, j: (i, 0)))
        args.append(comb)
    out_shape, out_specs = _cast_outputs(
        cast, jax.ShapeDtypeStruct((m, n_out), BF16), pl.BlockSpec((bm, bn), lambda i, j: (i, j)),
        [w_shape, w_shape], [wo_spec, wo_spec])
    return pl.pallas_call(
        functools.partial(_mm_swiglu_kernel, tiles_per_expert=nt if moe else 0, cast=cast),
        out_shape=out_shape,
        grid=(m // bm, n_out // bn),
        in_specs=in_specs,
        out_specs=out_specs,
        compiler_params=_params(2),
        name="mm_swiglu",
    )(*args)


def _mm_res_kernel(x_ref, w_ref, res_ref, gate_ref, o_ref, *wout):
    w = _weight_tile(w_ref, wout[0] if wout else None)
    acc = jnp.dot(x_ref[...], w, preferred_element_type=F32)
    o_ref[...] = res_ref[...] + gate_ref[...] * acc


def _mm_res(x, w, layer, res, gate, rows_per_batch, cast=False):
    m, k = x.shape
    n = w.shape[2]
    bm, bn = _tile(rows_per_batch, 1024), _tile(n, 512)
    assert not cast or m == bm
    tiles_per_batch = rows_per_batch // bm
    out_shape, out_specs = _cast_outputs(
        cast, jax.ShapeDtypeStruct((m, n), F32), pl.BlockSpec((bm, bn), lambda i, j: (i, j)),
        [(1, k, n)], [pl.BlockSpec((None, k, bn), lambda i, j: (0, 0, j))])
    return pl.pallas_call(
        _mm_res_kernel,
        out_shape=out_shape,
        grid=(m // bm, n // bn),
        in_specs=[
            pl.BlockSpec((bm, k), lambda i, j: (i, 0)),
            pl.BlockSpec((None, k, bn), lambda i, j: (layer, 0, j)),
            pl.BlockSpec((bm, bn), lambda i, j: (i, j)),
            pl.BlockSpec((None, 1, bn), lambda i, j: (i // tiles_per_batch, 0, j)),
        ],
        out_specs=out_specs,
        compiler_params=_params(2),
        name="mm_res",
    )(x, w, res, gate)


def _softmax_pv(scores, values):
    m = functools.reduce(jnp.maximum, [jnp.max(s, axis=-1, keepdims=True) for s in scores])
    ps = [jnp.exp2(s - m) for s in scores]
    denom = functools.reduce(jnp.add, [jnp.sum(p, axis=-1, keepdims=True) for p in ps])
    out = functools.reduce(
        jnp.add,
        [jnp.dot(p.astype(BF16), v, preferred_element_type=F32) for p, v in zip(ps, values)])
    return out / denom


def _qk(q, k):
    return lax.dot_general(q, k, (((1,), (1,)), ((), ())), preferred_element_type=F32)


NA_Q_ROWS = 4
NA_KEY_ROWS = 12


def _na_block_geometry(t, rows, win_r):
    u0 = min(max(t * NA_Q_ROWS - win_r // 2, 0), rows - NA_KEY_ROWS)
    geo = []
    for a in range(NA_Q_ROWS):
        r = t * NA_Q_ROWS + a
        r0 = min(max(r - win_r // 2, 0), rows - win_r)
        geo.append([(r0 <= u0 + j < r0 + win_r, u0 + j - r + win_r - 1) for j in range(NA_KEY_ROWS)])
    return u0, geo


def _na_build_slabs(pairs_ref, slab_ref, rows, win_r):
    n_blk = rows // NA_Q_ROWS
    lane = lax.broadcasted_iota(jnp.int32, (GRID_W, 2 * GRID_W), 1)
    neg = jnp.full((GRID_W, 2 * GRID_W), NEG_INF, F32)
    for v, t in enumerate((0, 1, n_blk - 1)):
        _, geo = _na_block_geometry(t, rows, win_r)
        for a in range(NA_Q_ROWS):
            for jj in range(NA_KEY_ROWS // 2):
                (ok0, d0), (ok1, _) = geo[a][2 * jj], geo[a][2 * jj + 1]
                if ok0 and ok1:
                    tile = pairs_ref[d0 + 1]
                elif ok0:
                    tile = jnp.where(lane < GRID_W, pairs_ref[d0 + 1], neg)
                elif ok1:
                    tile = jnp.where(lane >= GRID_W, pairs_ref[d0 + 1], neg)
                else:
                    tile = neg
                slab_ref[v, a * GRID_W:(a + 1) * GRID_W, jj * 2 * GRID_W:(jj + 1) * 2 * GRID_W] = tile


def _na_attn_kernel(q_ref, k_ref, v_ref, qc_ref, kc_ref, vc_ref, pairs_ref, o_ref, oc_ref, bias_ref,
                    *, rows, win_r, scale):
    @pl.when(pl.program_id(1) == 0)
    def _():
        _na_build_slabs(pairs_ref, bias_ref, rows, win_r)

    kc = kc_ref[...]
    vc = vc_ref[...]
    n_blk = rows // NA_Q_ROWS
    n_q = NA_Q_ROWS * GRID_W
    n_k = NA_KEY_ROWS * GRID_W

    def query_block(t, carry):
        u0 = jnp.clip(t * NA_Q_ROWS - win_r // 2, 0, rows - NA_KEY_ROWS)
        slab = jnp.where(t == 0, 0, jnp.where(t == n_blk - 1, 2, 1))
        q_rows = pl.ds(pl.multiple_of(t * n_q, n_q), n_q)
        k_rows = pl.ds(pl.multiple_of(u0 * GRID_W, GRID_W), n_k)
        q = q_ref[q_rows, :]
        s_win = _qk(q, k_ref[k_rows, :]) * scale + bias_ref[slab]
        s_ctx = _qk(q, kc) * scale
        o = _softmax_pv([s_win, s_ctx], [v_ref[k_rows, :], vc])
        o_ref[q_rows, :] = o.astype(BF16)
        return carry

    lax.fori_loop(0, n_blk, query_block, 0, unroll=4)
    oc_ref[...] = _softmax_pv([_qk(qc_ref[...], kc) * scale], [vc]).astype(BF16)


def _na_bias_pairs(rpb):
    n_heads, n_dr, n_dc = rpb.shape
    win_c = (n_dc + 1) // 2
    cols = jnp.arange(GRID_W)
    c0 = jnp.clip(cols - win_c // 2, 0, GRID_W - win_c)
    col_in = (cols[None, :] >= c0[:, None]) & (cols[None, :] < c0[:, None] + win_c)
    dc = jnp.clip(cols[None, :] - cols[:, None] + win_c - 1, 0, 2 * win_c - 2)
    masked = jnp.where(col_in, rpb[:, :, dc].astype(F32) * LOG2E, NEG_INF)
    neg = jnp.full((n_heads, 1, GRID_W, GRID_W), NEG_INF, F32)
    ext = jnp.concatenate([neg, masked, neg], axis=1)
    return jnp.concatenate([ext[:, :-1], ext[:, 1:]], axis=-1)


def _na_attention(qkv, qkvc, rpb, batch, seq, ctx_len, hd):
    n_heads = rpb.shape[0]
    win_r = (rpb.shape[1] + 1) // 2
    rows = seq // GRID_W
    n_blk = rows // NA_Q_ROWS
    assert rows % NA_Q_ROWS == 0 and n_blk >= 3 and NA_Q_ROWS >= win_r // 2
    assert (n_blk - 2) * NA_Q_ROWS - win_r // 2 <= rows - NA_KEY_ROWS
    assert NA_KEY_ROWS >= win_r + NA_Q_ROWS - 1 and NA_KEY_ROWS % 2 == 0
    d = n_heads * hd
    pairs = _na_bias_pairs(rpb)

    def spec(n_rows, part):
        return pl.BlockSpec((n_rows, hd), lambda h, b: (b, part * n_heads + h))

    return pl.pallas_call(
        functools.partial(_na_attn_kernel, rows=rows, win_r=win_r, scale=hd ** -0.5 * LOG2E),
        out_shape=(jax.ShapeDtypeStruct((batch * seq, d), BF16),
                   jax.ShapeDtypeStruct((batch * ctx_len, d), BF16)),
        grid=(n_heads, batch),
        in_specs=[spec(seq, 0), spec(seq, 1), spec(seq, 2),
                  spec(ctx_len, 0), spec(ctx_len, 1), spec(ctx_len, 2),
                  pl.BlockSpec((None,) + pairs.shape[1:], lambda h, b: (h, 0, 0, 0))],
        out_specs=(pl.BlockSpec((seq, hd), lambda h, b: (b, h)),
                   pl.BlockSpec((ctx_len, hd), lambda h, b: (b, h))),
        scratch_shapes=[pltpu.VMEM((3, NA_Q_ROWS * GRID_W, NA_KEY_ROWS * GRID_W), F32)],
        compiler_params=_params(2),
        name="na_attention",
    )(qkv, qkv, qkv, qkvc, qkvc, qkvc, pairs)


def _kv_outer(k, v):
    return lax.dot_general(k, v, (((0,), (0,)), ((), ())), preferred_element_type=F32)


def _retention_kernel(lgf_ref, lgb_ref, q_ref, k_ref, v_ref, g_ref, gnw_ref, *rest,
                      n_chunks, ctx_len):
    if ctx_len:
        kc_ref, vc_ref, o_ref, acc_ref, sf_ref, sb_ref = rest
    else:
        o_ref, acc_ref, sf_ref, sb_ref = rest
    c_len = RET_CHUNK
    h = pl.program_id(0)
    lgf = jnp.full((1, 1), lgf_ref[h], F32)
    lgb = jnp.full((1, 1), lgb_ref[h], F32)

    row = lax.broadcasted_iota(jnp.int32, (c_len, c_len), 0)
    col = lax.broadcasted_iota(jnp.int32, (c_len, c_len), 1)
    diff = (row - col).astype(F32)
    intra = jnp.where(diff >= 0, jnp.exp(lgf * jnp.maximum(diff, 0.0)),
                      jnp.exp(lgb * jnp.maximum(-diff, 0.0)))
    pos = lax.broadcasted_iota(jnp.int32, (c_len, 1), 0).astype(F32)
    q_dec_f = jnp.exp(lgf * (pos + 1.0))
    k_dec_f = jnp.exp(lgf * (c_len - 1.0 - pos))
    chunk_dec_f = jnp.exp(lgf * c_len)
    q_dec_b = jnp.exp(lgb * (c_len - pos))
    k_dec_b = jnp.exp(lgb * pos)
    chunk_dec_b = jnp.exp(lgb * c_len)

    if ctx_len:
        cpos = lax.broadcasted_iota(jnp.int32, (ctx_len, 1), 0).astype(F32)
        kc = kc_ref[...].astype(F32)
        vc = vc_ref[...]
        sf_ref[...] = _kv_outer((kc * jnp.exp(lgf * (ctx_len - 1.0 - cpos))).astype(BF16), vc)
        sb_ref[...] = _kv_outer((kc * jnp.exp(lgb * cpos)).astype(BF16), vc)
    else:
        sf_ref[...] = jnp.zeros_like(sf_ref)
        sb_ref[...] = jnp.zeros_like(sb_ref)

    def chunk_slice(c):
        return pl.ds(pl.multiple_of(c * c_len, c_len), c_len)

    def fwd_part(sl):
        q = q_ref[sl, :]
        k = k_ref[sl, :]
        v = v_ref[sl, :]
        s = (_qk(q, k) * intra).astype(BF16)
        qf = (q.astype(F32) * q_dec_f).astype(BF16)
        val = (jnp.dot(s, v, preferred_element_type=F32)
               + jnp.dot(qf, sf_ref[...].astype(BF16), preferred_element_type=F32))
        kf = (k.astype(F32) * k_dec_f).astype(BF16)
        sf_ref[...] = sf_ref[...] * chunk_dec_f + _kv_outer(kf, v)
        return val

    def bwd_part(sl):
        q = q_ref[sl, :]
        k = k_ref[sl, :]
        v = v_ref[sl, :]
        qb = (q.astype(F32) * q_dec_b).astype(BF16)
        val = jnp.dot(qb, sb_ref[...].astype(BF16), preferred_element_type=F32)
        kb = (k.astype(F32) * k_dec_b).astype(BF16)
        sb_ref[...] = sb_ref[...] * chunk_dec_b + _kv_outer(kb, v)
        return val

    def finish(sl, o):
        mu = jnp.mean(o, axis=-1, keepdims=True)
        cen = o - mu
        var = jnp.mean(cen * cen, axis=-1, keepdims=True)
        normed = cen * lax.rsqrt(var + EPS)
        o_ref[sl, :] = (normed * gnw_ref[...] * g_ref[sl, :].astype(F32)).astype(BF16)

    half = n_chunks // 2

    def first_half(t, carry):
        lo, hi = chunk_slice(t), chunk_slice(n_chunks - 1 - t)
        acc_ref[lo, :] = fwd_part(lo)
        acc_ref[hi, :] = bwd_part(hi)
        return carry

    def second_half(t, carry):
        hi, lo = chunk_slice(half + t), chunk_slice(half - 1 - t)
        finish(hi, acc_ref[hi, :] + fwd_part(hi))
        finish(lo, acc_ref[lo, :] + bwd_part(lo))
        return carry

    unroll = 2 if half % 2 == 0 else 1
    lax.fori_loop(0, half, first_half, 0, unroll=unroll)
    lax.fori_loop(0, half, second_half, 0, unroll=unroll)


def _retention(qkvg, lg_f, lg_b, gn_w, batch, n_tok, hd, qkvg_ctx=None, ctx_len=0):
    n_heads = lg_f.shape[0]
    d = n_heads * hd
    n_chunks = n_tok // RET_CHUNK
    assert n_tok % (2 * RET_CHUNK) == 0

    def spec(n_rows, part):
        return pl.BlockSpec((n_rows, hd), lambda h, b: (b, part * n_heads + h))

    smem = pl.BlockSpec(memory_space=pltpu.SMEM)
    in_specs = [smem, smem, spec(n_tok, 0), spec(n_tok, 1), spec(n_tok, 2), spec(n_tok, 3),
                pl.BlockSpec((1, hd), lambda h, b: (0, h))]
    args = [lg_f, lg_b, qkvg, qkvg, qkvg, qkvg, gn_w.reshape(1, d)]
    if ctx_len:
        in_specs += [spec(ctx_len, 1), spec(ctx_len, 2)]
        args += [qkvg_ctx, qkvg_ctx]
    return pl.pallas_call(
        functools.partial(_retention_kernel, n_chunks=n_chunks, ctx_len=ctx_len),
        out_shape=jax.ShapeDtypeStruct((batch * n_tok, d), BF16),
        grid=(n_heads, batch),
        in_specs=in_specs,
        out_specs=pl.BlockSpec((n_tok, hd), lambda h, b: (b, h)),
        scratch_shapes=[pltpu.VMEM((n_tok, hd), F32),
                        pltpu.VMEM((hd, hd), F32),
                        pltpu.VMEM((hd, hd), F32)],
        compiler_params=_params(2),
        name="retention",
    )(*args)


def _rope_tables(seq, hd):
    t = jnp.arange(seq)
    row = (t // GRID_W).astype(F32)
    col = (t % GRID_W).astype(F32)
    axis_dim = hd // 2
    inv_freq = jnp.power(ROPE_BASE, -jnp.arange(0, axis_dim, 2, dtype=F32) / axis_dim)
    ang_r = row[:, None] * inv_freq[None, :]
    ang_c = col[:, None] * inv_freq[None, :]
    cos = jnp.concatenate([jnp.cos(ang_r), jnp.cos(ang_r), jnp.cos(ang_c), jnp.cos(ang_c)], axis=-1)
    sin = jnp.concatenate([-jnp.sin(ang_r), jnp.sin(ang_r), -jnp.sin(ang_c), jnp.sin(ang_c)], axis=-1)
    return cos, sin


def kernel(x, c, ctx, c_ctx, ada_w, ada_b, norm1_w, norm2_w, na_wqkv, na_wo, na_qnorm_w, na_knorm_w, na_rpb, ret_wqkvg, ret_wo, ret_decay_f, ret_decay_b, ret_gn_w, ffn_w13, ffn_w2, moe_router, moe_w13, moe_w2):
    batch, seq, d = x.shape
    ctx_len = ctx.shape[1]
    n_ctx = batch * ctx_len
    depth = ada_w.shape[0]
    na_hd = na_qnorm_w.shape[1]
    ret_heads = ret_decay_f.shape[1]
    ret_hd = d // ret_heads
    assert batch + 1 <= ADA_ROWS

    cond = jnp.zeros((ADA_ROWS, d), F32).at[:batch].set(c).at[batch].set(c_ctx)
    ada = _ada_params(cond, ada_w, ada_b)
    rope = _rope_tables(seq, ret_hd)
    moe_w2_flat = moe_w2.reshape(moe_w2.shape[0], -1, d)

    h = x.reshape(batch * seq, d)
    hc = ctx.reshape(n_ctx, d)
    for i in range(depth):
        j = i // 2
        ctx_out = i < depth - 1
        mods = ada[i].reshape(ADA_ROWS, 6, 1, d)
        sh1, sc1, g1, sh2, sc2, g2 = (mods[:batch, p] for p in range(6))
        csh1, csc1, cg1, csh2, csc2, cg2 = (mods[batch:batch + 1, p] for p in range(6))

        def bf16_layer(w):
            return w[j:j + 1].astype(BF16)

        u = _norm_mod(h, norm1_w[i], sh1, sc1, seq)
        uc = _norm_mod(hc, norm1_w[i], csh1, csc1, n_ctx)
        if i % 2 == 0:
            w_o = na_wo
            qkvc, w_qkv = _mm_na_qkv(uc, na_wqkv, j, na_qnorm_w[j], na_knorm_w[j], cast=True)
            qkv = _mm_na_qkv(u, w_qkv, 0, na_qnorm_w[j], na_knorm_w[j])
            o, oc = _na_attention(qkv, qkvc, na_rpb[j], batch, seq, ctx_len, na_hd)
        else:
            w_o = ret_wo
            lg_f = jax.nn.log_sigmoid(ret_decay_f[j].astype(F32))
            lg_b = jax.nn.log_sigmoid(ret_decay_b[j].astype(F32))
            qkvgc, w_qkvg = _mm_ret_qkvg(uc, ret_wqkvg, j, ret_hd, None, n_ctx, cast=True)
            qkvg = _mm_ret_qkvg(u, w_qkvg, 0, ret_hd, rope, seq)
            o = _retention(qkvg, lg_f, lg_b, ret_gn_w[j], batch, seq, ret_hd, qkvgc, ctx_len)
            if ctx_out:
                oc = _retention(qkvgc, lg_f, lg_b, ret_gn_w[j], batch, ctx_len, ret_hd)
        if ctx_out:
            hc, w_o_b = _mm_res(oc, w_o, j, hc, cg1, n_ctx, cast=True)
        else:
            w_o_b = bf16_layer(w_o)
        h = _mm_res(o, w_o_b, 0, h, g1, seq)

        if i % 2 == 0:
            w13, w2, router = ffn_w13, ffn_w2, None
        else:
            w13, w2, router = moe_w13, moe_w2_flat, moe_router[j]
        comb = combc = None
        if ctx_out:
            uc2 = _norm_mod(hc, norm2_w[i], csh2, csc2, n_ctx, router)
            if router is not None:
                uc2, combc = uc2
            actc, w_a, w_b = _mm_swiglu(uc2, w13, w13, j, combc, cast=True)
            hc, w2_b = _mm_res(actc, w2, j, hc, cg2, n_ctx, cast=True)
        else:
            half = w13.shape[-1] // 2
            w_a, w_b = bf16_layer(w13[..., :half]), bf16_layer(w13[..., half:])
            w2_b = bf16_layer(w2)
        u2 = _norm_mod(h, norm2_w[i], sh2, sc2, seq, router)
        if router is not None:
            u2, comb = u2
        h = _mm_res(_mm_swiglu(u2, w_a, w_b, 0, comb), w2_b, 0, h, g2, seq)
    return h.reshape(batch, seq, d)
```

```python
import functools

import jax
import jax.numpy as jnp
from jax import lax
from jax.experimental import pallas as pl
from jax.experimental.pallas import tpu as pltpu

F32 = jnp.float32
BF16 = jnp.bfloat16

GRID_W = 64
RET_CHUNK = 128
ROPE_BASE = 10000.0
EPS = 1e-6
NEG_INF = -1e30
LOG2E = 1.4426950408889634

V7X_VMEM_BYTES = 64 * 1024 * 1024
VMEM_LIMIT_BYTES = V7X_VMEM_BYTES * 7 // 8
LANES = 128
ADA_ROWS = 8


def _params(n_grid_dims):
    return pltpu.CompilerParams(
        dimension_semantics=("arbitrary",) * n_grid_dims,
        vmem_limit_bytes=VMEM_LIMIT_BYTES,
    )


def _tile(n, pref):
    t = min(pref, n)
    while n % t:
        t //= 2
    return t


def _silu(x):
    return x * (1.0 / (1.0 + jnp.exp(-x)))


def _ada_kernel(x_ref, w_ref, b_ref, o_ref):
    xa = _silu(x_ref[...]).astype(BF16)
    w = w_ref[...].astype(BF16)
    o_ref[...] = jnp.dot(xa, w, preferred_element_type=F32) + b_ref[...]


def _ada_params(cond, ada_w, ada_b):
    depth, d, n = ada_w.shape
    bn = _tile(n, 512)
    return pl.pallas_call(
        _ada_kernel,
        out_shape=jax.ShapeDtypeStruct((depth, ADA_ROWS, n), F32),
        grid=(depth, n // bn),
        in_specs=[
            pl.BlockSpec((ADA_ROWS, d), lambda l, j: (0, 0)),
            pl.BlockSpec((None, d, bn), lambda l, j: (l, 0, j)),
            pl.BlockSpec((None, 1, bn), lambda l, j: (l, 0, j)),
        ],
        out_specs=pl.BlockSpec((None, ADA_ROWS, bn), lambda l, j: (l, 0, j)),
        compiler_params=_params(2),
        name="ada_params",
    )(cond, ada_w, ada_b.reshape(depth, 1, n))


def _norm_mod_kernel(x_ref, nw_ref, sh_ref, sc_ref, *rest, n_experts):
    x = x_ref[...]
    y = x * lax.rsqrt(jnp.mean(x * x, axis=-1, keepdims=True) + EPS)
    u = (y * nw_ref[...]) * (1.0 + sc_ref[...]) + sh_ref[...]
    ub = u.astype(BF16)
    if n_experts == 0:
        (u_ref,) = rest
        u_ref[...] = ub
        return
    wr_ref, u_ref, comb_ref = rest
    u_ref[...] = ub
    logits = jnp.dot(ub, wr_ref[...], preferred_element_type=F32)
    lane = lax.broadcasted_iota(jnp.int32, logits.shape, 1).astype(F32)
    valid = lane < n_experts
    logits = jnp.where(valid, logits, NEG_INF)
    e = jnp.exp(logits - jnp.max(logits, axis=-1, keepdims=True))
    probs = jnp.where(valid, e / jnp.sum(e, axis=-1, keepdims=True), -1.0)
    p1 = jnp.max(probs, axis=-1, keepdims=True)
    i1 = jnp.min(jnp.where(probs == p1, lane, float(LANES)), axis=-1, keepdims=True)
    rest_p = jnp.where(lane == i1, -1.0, probs)
    p2 = jnp.max(rest_p, axis=-1, keepdims=True)
    i2 = jnp.min(jnp.where(rest_p == p2, lane, float(LANES)), axis=-1, keepdims=True)
    inv = 1.0 / (p1 + p2)
    comb_ref[...] = jnp.where(lane == i1, p1 * inv, jnp.where(lane == i2, p2 * inv, 0.0))


def _norm_mod(x, norm_w, shift, scale, rows_per_batch, router_w=None):
    m, d = x.shape
    bm = _tile(rows_per_batch, 512)
    tiles_per_batch = rows_per_batch // bm
    mod_spec = pl.BlockSpec((None, 1, d), lambda i: (i // tiles_per_batch, 0, 0))
    in_specs = [
        pl.BlockSpec((bm, d), lambda i: (i, 0)),
        pl.BlockSpec((1, d), lambda i: (0, 0)),
        mod_spec,
        mod_spec,
    ]
    args = [x, norm_w.reshape(1, d), shift, scale]
    u_shape = jax.ShapeDtypeStruct((m, d), BF16)
    u_spec = pl.BlockSpec((bm, d), lambda i: (i, 0))
    if router_w is None:
        n_experts = 0
        out_shape, out_specs = u_shape, u_spec
    else:
        n_experts = router_w.shape[1]
        wr = jnp.zeros((d, LANES), BF16).at[:, :n_experts].set(router_w.astype(BF16))
        in_specs.append(pl.BlockSpec((d, LANES), lambda i: (0, 0)))
        args.append(wr)
        out_shape = (u_shape, jax.ShapeDtypeStruct((m, LANES), F32))
        out_specs = (u_spec, pl.BlockSpec((bm, LANES), lambda i: (i, 0)))
    return pl.pallas_call(
        functools.partial(_norm_mod_kernel, n_experts=n_experts),
        out_shape=out_shape,
        grid=(m // bm,),
        in_specs=in_specs,
        out_specs=out_specs,
        compiler_params=_params(1),
        name="norm_mod",
    )(*args)


def _weight_tile(w_ref, wout_ref):
    if wout_ref is None:
        return w_ref[...]
    w = w_ref[...].astype(BF16)
    wout_ref[...] = w
    return w


def _mm_na_qkv_kernel(x_ref, w_ref, qn_ref, kn_ref, o_ref, *wout, tiles_per_part, hd):
    j = pl.program_id(1)
    w = _weight_tile(w_ref, wout[0] if wout else None)
    acc = jnp.dot(x_ref[...], w, preferred_element_type=F32)
    n_heads = acc.shape[1] // hd

    def store_normed(nw, scale):
        for t in range(n_heads):
            a = acc[:, t * hd:(t + 1) * hd]
            y = a * lax.rsqrt(jnp.mean(a * a, axis=-1, keepdims=True) + EPS)
            o_ref[:, t * hd:(t + 1) * hd] = ((y * nw) * scale).astype(BF16)

    @pl.when(j < tiles_per_part)
    def _():
        store_normed(qn_ref[...], hd ** -0.5 * LOG2E)

    @pl.when((j >= tiles_per_part) & (j < 2 * tiles_per_part))
    def _():
        store_normed(kn_ref[...], 1.0)

    @pl.when(j >= 2 * tiles_per_part)
    def _():
        o_ref[...] = acc.astype(BF16)


def _cast_outputs(cast, out_shape, out_spec, w_shapes, w_specs):
    if not cast:
        return out_shape, out_spec
    return ((out_shape,) + tuple(jax.ShapeDtypeStruct(s, BF16) for s in w_shapes),
            (out_spec,) + tuple(w_specs))


def _mm_na_qkv(x, w, layer, qn_w, kn_w, cast=False):
    m, k = x.shape
    n = w.shape[2]
    hd = qn_w.shape[0]
    bm, bn = _tile(m, 1024), _tile(n // 3, 512 if cast else 1024)
    assert not cast or m == bm
    out_shape, out_specs = _cast_outputs(
        cast, jax.ShapeDtypeStruct((m, n), BF16), pl.BlockSpec((bm, bn), lambda i, j: (i, j)),
        [(1, k, n)], [pl.BlockSpec((None, k, bn), lambda i, j: (0, 0, j))])
    return pl.pallas_call(
        functools.partial(_mm_na_qkv_kernel, tiles_per_part=n // 3 // bn, hd=hd),
        out_shape=out_shape,
        grid=(m // bm, n // bn),
        in_specs=[
            pl.BlockSpec((bm, k), lambda i, j: (i, 0)),
            pl.BlockSpec((None, k, bn), lambda i, j: (layer, 0, j)),
            pl.BlockSpec((1, hd), lambda i, j: (0, 0)),
            pl.BlockSpec((1, hd), lambda i, j: (0, 0)),
        ],
        out_specs=out_specs,
        compiler_params=_params(2),
        name="mm_na_qkv",
    )(x, w, qn_w.reshape(1, hd), kn_w.reshape(1, hd))


def _mm_ret_qkvg_kernel(x_ref, w_ref, *rest, tiles_per_part, hd, rope, cast):
    if rope:
        cos_ref, sin_ref = rest[:2]
        rest = rest[2:]
    o_ref = rest[0]
    j = pl.program_id(1)
    w = _weight_tile(w_ref, rest[1] if cast else None)
    acc = jnp.dot(x_ref[...], w, preferred_element_type=F32)
    n_heads = acc.shape[1] // hd
    k_scale = hd ** -0.5

    def store_qk(scale):
        for t in range(n_heads):
            a = acc[:, t * hd:(t + 1) * hd]
            if rope:
                lane = lax.broadcasted_iota(jnp.int32, a.shape, 1)
                first = (lane % (hd // 2)) < (hd // 4)
                partner = jnp.where(first, pltpu.roll(a, hd - hd // 4, 1), pltpu.roll(a, hd // 4, 1))
                a = a * cos_ref[...] + partner * sin_ref[...]
            o_ref[:, t * hd:(t + 1) * hd] = (a * scale).astype(BF16)

    @pl.when(j < tiles_per_part)
    def _():
        store_qk(1.0)

    @pl.when((j >= tiles_per_part) & (j < 2 * tiles_per_part))
    def _():
        store_qk(k_scale)

    @pl.when((j >= 2 * tiles_per_part) & (j < 3 * tiles_per_part))
    def _():
        o_ref[...] = acc.astype(BF16)

    @pl.when(j >= 3 * tiles_per_part)
    def _():
        o_ref[...] = _silu(acc).astype(BF16)


def _mm_ret_qkvg(x, w, layer, hd, rope_tables, seq, cast=False):
    m, k = x.shape
    n = w.shape[2]
    bm, bn = _tile(seq, 1024), _tile(n // 4, 512 if cast else 1024)
    assert not cast or m == bm
    rope = rope_tables is not None
    in_specs = [
        pl.BlockSpec((bm, k), lambda i, j: (i, 0)),
        pl.BlockSpec((None, k, bn), lambda i, j: (layer, 0, j)),
    ]
    args = [x, w]
    if rope:
        tiles_per_seq = seq // bm
        tab_spec = pl.BlockSpec((bm, hd), lambda i, j: (i % tiles_per_seq, 0))
        in_specs += [tab_spec, tab_spec]
        args += list(rope_tables)
    out_shape, out_specs = _cast_outputs(
        cast, jax.ShapeDtypeStruct((m, n), BF16), pl.BlockSpec((bm, bn), lambda i, j: (i, j)),
        [(1, k, n)], [pl.BlockSpec((None, k, bn), lambda i, j: (0, 0, j))])
    return pl.pallas_call(
        functools.partial(_mm_ret_qkvg_kernel, tiles_per_part=n // 4 // bn, hd=hd, rope=rope, cast=cast),
        out_shape=out_shape,
        grid=(m // bm, n // bn),
        in_specs=in_specs,
        out_specs=out_specs,
        compiler_params=_params(2),
        name="mm_ret_qkvg",
    )(*args)


def _mm_swiglu_kernel(x_ref, wa_ref, wb_ref, *rest, tiles_per_expert, cast):
    if tiles_per_expert:
        comb_ref = rest[0]
        rest = rest[1:]
    o_ref = rest[0]
    x = x_ref[...]
    a = jnp.dot(x, _weight_tile(wa_ref, rest[1] if cast else None), preferred_element_type=F32)
    b = jnp.dot(x, _weight_tile(wb_ref, rest[2] if cast else None), preferred_element_type=F32)
    act = _silu(a) * b
    if tiles_per_expert:
        e = pl.program_id(1) // tiles_per_expert
        comb = comb_ref[...]
        lane = lax.broadcasted_iota(jnp.int32, comb.shape, 1)
        act = act * jnp.sum(jnp.where(lane == e, comb, 0.0), axis=-1, keepdims=True)
    o_ref[...] = act.astype(BF16)


def _mm_swiglu(x, wa, wb, layer, comb=None, cast=False):
    m, k = x.shape
    bm = _tile(m, 1024)
    assert not cast or m == bm
    moe = comb is not None
    f = wa.shape[-1] // 2 if cast else wa.shape[-1]
    bn = _tile(f, 256 if cast else 512)
    nt = f // bn
    b_off = nt if cast else 0
    if moe:
        n_e = wa.shape[1]
        wa_spec = pl.BlockSpec((None, None, k, bn), lambda i, j: (layer, j // nt, 0, j % nt))
        wb_spec = pl.BlockSpec((None, None, k, bn), lambda i, j: (layer, j // nt, 0, b_off + j % nt))
        w_shape = (1, n_e, k, f)
        wo_spec = pl.BlockSpec((None, None, k, bn), lambda i, j: (0, j // nt, 0, j % nt))
        n_out = n_e * f
    else:
        wa_spec = pl.BlockSpec((None, k, bn), lambda i, j: (layer, 0, j))
        wb_spec = pl.BlockSpec((None, k, bn), lambda i, j: (layer, 0, b_off + j))
        w_shape = (1, k, f)
        wo_spec = pl.BlockSpec((None, k, bn), lambda i, j: (0, 0, j))
        n_out = f
    in_specs = [pl.BlockSpec((bm, k), lambda i, j: (i, 0)), wa_spec, wb_spec]
    args = [x, wa, wb]
    if moe:
        in_specs.append(pl.BlockSpec((bm, LANES), lambda i, j: (i, 0)))
        args.append(comb)
    out_shape, out_specs = _cast_outputs(
        cast, jax.ShapeDtypeStruct((m, n_out), BF16), pl.BlockSpec((bm, bn), lambda i, j: (i, j)),
        [w_shape, w_shape], [wo_spec, wo_spec])
    return pl.pallas_call(
        functools.partial(_mm_swiglu_kernel, tiles_per_expert=nt if moe else 0, cast=cast),
        out_shape=out_shape,
        grid=(m // bm, n_out // bn),
        in_specs=in_specs,
        out_specs=out_specs,
        compiler_params=_params(2),
        name="mm_swiglu",
    )(*args)


def _mm_res_kernel(x_ref, w_ref, res_ref, gate_ref, o_ref, *wout):
    w = _weight_tile(w_ref, wout[0] if wout else None)
    acc = jnp.dot(x_ref[...], w, preferred_element_type=F32)
    o_ref[...] = res_ref[...] + gate_ref[...] * acc


def _mm_res(x, w, layer, res, gate, rows_per_batch, cast=False):
    m, k = x.shape
    n = w.shape[2]
    bm, bn = _tile(rows_per_batch, 1024), _tile(n, 512 if cast else 1024)
    assert not cast or m == bm
    tiles_per_batch = rows_per_batch // bm
    out_shape, out_specs = _cast_outputs(
        cast, jax.ShapeDtypeStruct((m, n), F32), pl.BlockSpec((bm, bn), lambda i, j: (i, j)),
        [(1, k, n)], [pl.BlockSpec((None, k, bn), lambda i, j: (0, 0, j))])
    return pl.pallas_call(
        _mm_res_kernel,
        out_shape=out_shape,
        grid=(m // bm, n // bn),
        in_specs=[
            pl.BlockSpec((bm, k), lambda i, j: (i, 0)),
            pl.BlockSpec((None, k, bn), lambda i, j: (layer, 0, j)),
            pl.BlockSpec((bm, bn), lambda i, j: (i, j)),
            pl.BlockSpec((None, 1, bn), lambda i, j: (i // tiles_per_batch, 0, j)),
        ],
        out_specs=out_specs,
        compiler_params=_params(2),
        name="mm_res",
    )(x, w, res, gate)


def _softmax_pv(scores, values):
    m = functools.reduce(jnp.maximum, [jnp.max(s, axis=-1, keepdims=True) for s in scores])
    out = functools.reduce(
        jnp.add,
        [jnp.dot(jnp.exp2(s - m).astype(BF16), v, preferred_element_type=F32)
         for s, v in zip(scores, values)])
    hd = out.shape[1] // 2
    return out[:, :hd] / out[:, hd:]


def _qk(q, k):
    return lax.dot_general(q, k, (((1,), (1,)), ((), ())), preferred_element_type=F32)


NA_Q_ROWS = 4
NA_KEY_ROWS = 12


def _na_block_geometry(t, rows, win_r):
    u0 = min(max(t * NA_Q_ROWS - win_r // 2, 0), rows - NA_KEY_ROWS)
    geo = []
    for a in range(NA_Q_ROWS):
        r = t * NA_Q_ROWS + a
        r0 = min(max(r - win_r // 2, 0), rows - win_r)
        geo.append([(r0 <= u0 + j < r0 + win_r, u0 + j - r + win_r - 1) for j in range(NA_KEY_ROWS)])
    return u0, geo


def _na_build_slabs(pairs_ref, slab_ref, rows, win_r):
    n_blk = rows // NA_Q_ROWS
    lane = lax.broadcasted_iota(jnp.int32, (GRID_W, 2 * GRID_W), 1)
    neg = jnp.full((GRID_W, 2 * GRID_W), NEG_INF, F32)
    for v, t in enumerate((0, 1, n_blk - 1)):
        _, geo = _na_block_geometry(t, rows, win_r)
        for a in range(NA_Q_ROWS):
            for jj in range(NA_KEY_ROWS // 2):
                (ok0, d0), (ok1, _) = geo[a][2 * jj], geo[a][2 * jj + 1]
                if ok0 and ok1:
                    tile = pairs_ref[d0 + 1]
                elif ok0:
                    tile = jnp.where(lane < GRID_W, pairs_ref[d0 + 1], neg)
                elif ok1:
                    tile = jnp.where(lane >= GRID_W, pairs_ref[d0 + 1], neg)
                else:
                    tile = neg
                slab_ref[v, a * GRID_W:(a + 1) * GRID_W, jj * 2 * GRID_W:(jj + 1) * 2 * GRID_W] = tile


def _na_attn_kernel(q_ref, k_ref, v_ref, qc_ref, kc_ref, vc_ref, pairs_ref, o_ref, oc_ref,
                    bias_ref, vaug_ref, vcaug_ref, *, rows, win_r):
    hd = v_ref.shape[1]

    @pl.when(pl.program_id(1) == 0)
    def _():
        _na_build_slabs(pairs_ref, bias_ref, rows, win_r)

    @pl.when((pl.program_id(0) == 0) & (pl.program_id(1) == 0))
    def _():
        vaug_ref[:, hd:] = jnp.ones((vaug_ref.shape[0], hd), BF16)
        vcaug_ref[:, hd:] = jnp.ones((vcaug_ref.shape[0], hd), BF16)

    vaug_ref[:, :hd] = v_ref[...]
    vcaug_ref[:, :hd] = vc_ref[...]
    kc = kc_ref[...]
    vc = vcaug_ref[...]
    n_blk = rows // NA_Q_ROWS
    n_q = NA_Q_ROWS * GRID_W
    n_k = NA_KEY_ROWS * GRID_W

    def query_block(t, carry):
        u0 = jnp.clip(t * NA_Q_ROWS - win_r // 2, 0, rows - NA_KEY_ROWS)
        slab = jnp.where(t == 0, 0, jnp.where(t == n_blk - 1, 2, 1))
        q_rows = pl.ds(pl.multiple_of(t * n_q, n_q), n_q)
        k_rows = pl.ds(pl.multiple_of(u0 * GRID_W, GRID_W), n_k)
        q = q_ref[q_rows, :]
        s_win = _qk(q, k_ref[k_rows, :]) + bias_ref[slab]
        o = _softmax_pv([s_win, _qk(q, kc)], [vaug_ref[k_rows, :], vc])
        o_ref[q_rows, :] = o.astype(BF16)
        return carry

    lax.fori_loop(0, n_blk, query_block, 0, unroll=4)
    oc_ref[...] = _softmax_pv([_qk(qc_ref[...], kc)], [vc]).astype(BF16)


def _na_bias_pairs(rpb):
    n_heads, n_dr, n_dc = rpb.shape
    win_c = (n_dc + 1) // 2
    cols = jnp.arange(GRID_W)
    c0 = jnp.clip(cols - win_c // 2, 0, GRID_W - win_c)
    col_in = (cols[None, :] >= c0[:, None]) & (cols[None, :] < c0[:, None] + win_c)
    dc = jnp.clip(cols[None, :] - cols[:, None] + win_c - 1, 0, 2 * win_c - 2)
    masked = jnp.where(col_in, rpb[:, :, dc].astype(F32) * LOG2E, NEG_INF)
    neg = jnp.full((n_heads, 1, GRID_W, GRID_W), NEG_INF, F32)
    ext = jnp.concatenate([neg, masked, neg], axis=1)
    return jnp.concatenate([ext[:, :-1], ext[:, 1:]], axis=-1)


def _na_attention(qkv, qkvc, rpb, batch, seq, ctx_len, hd):
    n_heads = rpb.shape[0]
    win_r = (rpb.shape[1] + 1) // 2
    rows = seq // GRID_W
    n_blk = rows // NA_Q_ROWS
    assert rows % NA_Q_ROWS == 0 and n_blk >= 3 and NA_Q_ROWS >= win_r // 2
    assert (n_blk - 2) * NA_Q_ROWS - win_r // 2 <= rows - NA_KEY_ROWS
    assert NA_KEY_ROWS >= win_r + NA_Q_ROWS - 1 and NA_KEY_ROWS % 2 == 0
    d = n_heads * hd
    pairs = _na_bias_pairs(rpb)

    def spec(n_rows, part):
        return pl.BlockSpec((n_rows, hd), lambda h, b: (b, part * n_heads + h))

    return pl.pallas_call(
        functools.partial(_na_attn_kernel, rows=rows, win_r=win_r),
        out_shape=(jax.ShapeDtypeStruct((batch * seq, d), BF16),
                   jax.ShapeDtypeStruct((batch * ctx_len, d), BF16)),
        grid=(n_heads, batch),
        in_specs=[spec(seq, 0), spec(seq, 1), spec(seq, 2),
                  spec(ctx_len, 0), spec(ctx_len, 1), spec(ctx_len, 2),
                  pl.BlockSpec((None,) + pairs.shape[1:], lambda h, b: (h, 0, 0, 0))],
        out_specs=(pl.BlockSpec((seq, hd), lambda h, b: (b, h)),
                   pl.BlockSpec((ctx_len, hd), lambda h, b: (b, h))),
        scratch_shapes=[pltpu.VMEM((3, NA_Q_ROWS * GRID_W, NA_KEY_ROWS * GRID_W), F32),
                        pltpu.VMEM((seq, 2 * hd), BF16),
                        pltpu.VMEM((ctx_len, 2 * hd), BF16)],
        compiler_params=_params(2),
        name="na_attention",
    )(qkv, qkv, qkv, qkvc, qkvc, qkvc, pairs)


def _kv_outer(k, v):
    return lax.dot_general(k, v, (((0,), (0,)), ((), ())), preferred_element_type=F32)


def _retention_kernel(lgf_ref, lgb_ref, q_ref, k_ref, v_ref, g_ref, gnw_ref, *rest,
                      n_chunks, ctx_len):
    if ctx_len:
        kc_ref, vc_ref, o_ref, acc_ref, sf_ref, sb_ref = rest
    else:
        o_ref, acc_ref, sf_ref, sb_ref = rest
    c_len = RET_CHUNK
    h = pl.program_id(0)
    lgf = jnp.full((1, 1), lgf_ref[h], F32)
    lgb = jnp.full((1, 1), lgb_ref[h], F32)

    row = lax.broadcasted_iota(jnp.int32, (c_len, c_len), 0)
    col = lax.broadcasted_iota(jnp.int32, (c_len, c_len), 1)
    diff = (row - col).astype(F32)
    intra = jnp.where(diff >= 0, jnp.exp(lgf * jnp.maximum(diff, 0.0)),
                      jnp.exp(lgb * jnp.maximum(-diff, 0.0)))
    pos = lax.broadcasted_iota(jnp.int32, (c_len, 1), 0).astype(F32)
    q_dec_f = jnp.exp(lgf * (pos + 1.0))
    k_dec_f = jnp.exp(lgf * (c_len - 1.0 - pos))
    chunk_dec_f = jnp.exp(lgf * c_len)
    q_dec_b = jnp.exp(lgb * (c_len - pos))
    k_dec_b = jnp.exp(lgb * pos)
    chunk_dec_b = jnp.exp(lgb * c_len)

    if ctx_len:
        cpos = lax.broadcasted_iota(jnp.int32, (ctx_len, 1), 0).astype(F32)
        kc = kc_ref[...].astype(F32)
        vc = vc_ref[...]
        sf_ref[...] = _kv_outer((kc * jnp.exp(lgf * (ctx_len - 1.0 - cpos))).astype(BF16), vc)
        sb_ref[...] = _kv_outer((kc * jnp.exp(lgb * cpos)).astype(BF16), vc)
    else:
        sf_ref[...] = jnp.zeros_like(sf_ref)
        sb_ref[...] = jnp.zeros_like(sb_ref)

    def chunk_slice(c):
        return pl.ds(pl.multiple_of(c * c_len, c_len), c_len)

    def fwd_part(sl):
        q = q_ref[sl, :]
        k = k_ref[sl, :]
        v = v_ref[sl, :]
        s = (_qk(q, k) * intra).astype(BF16)
        qf = (q.astype(F32) * q_dec_f).astype(BF16)
        val = (jnp.dot(s, v, preferred_element_type=F32)
               + jnp.dot(qf, sf_ref[...].astype(BF16), preferred_element_type=F32))
        kf = (k.astype(F32) * k_dec_f).astype(BF16)
        sf_ref[...] = sf_ref[...] * chunk_dec_f + _kv_outer(kf, v)
        return val

    def bwd_part(sl):
        q = q_ref[sl, :]
        k = k_ref[sl, :]
        v = v_ref[sl, :]
        qb = (q.astype(F32) * q_dec_b).astype(BF16)
        val = jnp.dot(qb, sb_ref[...].astype(BF16), preferred_element_type=F32)
        kb = (k.astype(F32) * k_dec_b).astype(BF16)
        sb_ref[...] = sb_ref[...] * chunk_dec_b + _kv_outer(kb, v)
        return val

    def finish(sl, o):
        mu = jnp.mean(o, axis=-1, keepdims=True)
        cen = o - mu
        var = jnp.mean(cen * cen, axis=-1, keepdims=True)
        normed = cen * lax.rsqrt(var + EPS)
        o_ref[sl, :] = (normed * gnw_ref[...] * g_ref[sl, :].astype(F32)).astype(BF16)

    half = n_chunks // 2

    def first_half(t, carry):
        lo, hi = chunk_slice(t), chunk_slice(n_chunks - 1 - t)
        acc_ref[lo, :] = fwd_part(lo)
        acc_ref[hi, :] = bwd_part(hi)
        return carry

    def second_half(t, carry):
        hi, lo = chunk_slice(half + t), chunk_slice(half - 1 - t)
        finish(hi, acc_ref[hi, :] + fwd_part(hi))
        finish(lo, acc_ref[lo, :] + bwd_part(lo))
        return carry

    unroll = 2 if half % 2 == 0 else 1
    lax.fori_loop(0, half, first_half, 0, unroll=unroll)
    lax.fori_loop(0, half, second_half, 0, unroll=unroll)


def _retention(qkvg, lg_f, lg_b, gn_w, batch, n_tok, hd, qkvg_ctx=None, ctx_len=0):
    n_heads = lg_f.shape[0]
    d = n_heads * hd
    n_chunks = n_tok // RET_CHUNK
    assert n_tok % (2 * RET_CHUNK) == 0

    def spec(n_rows, part):
        return pl.BlockSpec((n_rows, hd), lambda h, b: (b, part * n_heads + h))

    smem = pl.BlockSpec(memory_space=pltpu.SMEM)
    in_specs = [smem, smem, spec(n_tok, 0), spec(n_tok, 1), spec(n_tok, 2), spec(n_tok, 3),
                pl.BlockSpec((1, hd), lambda h, b: (0, h))]
    args = [lg_f, lg_b, qkvg, qkvg, qkvg, qkvg, gn_w.reshape(1, d)]
    if ctx_len:
        in_specs += [spec(ctx_len, 1), spec(ctx_len, 2)]
        args += [qkvg_ctx, qkvg_ctx]
    return pl.pallas_call(
        functools.partial(_retention_kernel, n_chunks=n_chunks, ctx_len=ctx_len),
        out_shape=jax.ShapeDtypeStruct((batch * n_tok, d), BF16),
        grid=(n_heads, batch),
        in_specs=in_specs,
        out_specs=pl.BlockSpec((n_tok, hd), lambda h, b: (b, h)),
        scratch_shapes=[pltpu.VMEM((n_tok, hd), F32),
                        pltpu.VMEM((hd, hd), F32),
                        pltpu.VMEM((hd, hd), F32)],
        compiler_params=_params(2),
        name="retention",
    )(*args)


def _rope_tables(seq, hd):
    t = jnp.arange(seq)
    row = (t // GRID_W).astype(F32)
    col = (t % GRID_W).astype(F32)
    axis_dim = hd // 2
    inv_freq = jnp.power(ROPE_BASE, -jnp.arange(0, axis_dim, 2, dtype=F32) / axis_dim)
    ang_r = row[:, None] * inv_freq[None, :]
    ang_c = col[:, None] * inv_freq[None, :]
    cos = jnp.concatenate([jnp.cos(ang_r), jnp.cos(ang_r), jnp.cos(ang_c), jnp.cos(ang_c)], axis=-1)
    sin = jnp.concatenate([-jnp.sin(ang_r), jnp.sin(ang_r), -jnp.sin(ang_c), jnp.sin(ang_c)], axis=-1)
    return cos, sin


def kernel(x, c, ctx, c_ctx, ada_w, ada_b, norm1_w, norm2_w, na_wqkv, na_wo, na_qnorm_w, na_knorm_w, na_rpb, ret_wqkvg, ret_wo, ret_decay_f, ret_decay_b, ret_gn_w, ffn_w13, ffn_w2, moe_router, moe_w13, moe_w2):
    batch, seq, d = x.shape
    ctx_len = ctx.shape[1]
    n_ctx = batch * ctx_len
    depth = ada_w.shape[0]
    na_hd = na_qnorm_w.shape[1]
    ret_heads = ret_decay_f.shape[1]
    ret_hd = d // ret_heads
    assert batch + 1 <= ADA_ROWS

    cond = jnp.zeros((ADA_ROWS, d), F32).at[:batch].set(c).at[batch].set(c_ctx)
    ada = _ada_params(cond, ada_w, ada_b)
    rope = _rope_tables(seq, ret_hd)
    moe_w2_flat = moe_w2.reshape(moe_w2.shape[0], -1, d)

    h = x.reshape(batch * seq, d)
    hc = ctx.reshape(n_ctx, d)
    for i in range(depth):
        j = i // 2
        ctx_out = i < depth - 1
        mods = ada[i].reshape(ADA_ROWS, 6, 1, d)
        sh1, sc1, g1, sh2, sc2, g2 = (mods[:batch, p] for p in range(6))
        csh1, csc1, cg1, csh2, csc2, cg2 = (mods[batch:batch + 1, p] for p in range(6))

        def bf16_layer(w):
            return w[j:j + 1].astype(BF16)

        u = _norm_mod(h, norm1_w[i], sh1, sc1, seq)
        uc = _norm_mod(hc, norm1_w[i], csh1, csc1, n_ctx)
        if i % 2 == 0:
            w_o = na_wo
            qkvc, w_qkv = _mm_na_qkv(uc, na_wqkv, j, na_qnorm_w[j], na_knorm_w[j], cast=True)
            qkv = _mm_na_qkv(u, w_qkv, 0, na_qnorm_w[j], na_knorm_w[j])
            o, oc = _na_attention(qkv, qkvc, na_rpb[j], batch, seq, ctx_len, na_hd)
        else:
            w_o = ret_wo
            lg_f = jax.nn.log_sigmoid(ret_decay_f[j].astype(F32))
            lg_b = jax.nn.log_sigmoid(ret_decay_b[j].astype(F32))
            qkvgc, w_qkvg = _mm_ret_qkvg(uc, ret_wqkvg, j, ret_hd, None, n_ctx, cast=True)
            qkvg = _mm_ret_qkvg(u, w_qkvg, 0, ret_hd, rope, seq)
            o = _retention(qkvg, lg_f, lg_b, ret_gn_w[j], batch, seq, ret_hd, qkvgc, ctx_len)
            if ctx_out:
                oc = _retention(qkvgc, lg_f, lg_b, ret_gn_w[j], batch, ctx_len, ret_hd)
        if ctx_out:
            hc, w_o_b = _mm_res(oc, w_o, j, hc, cg1, n_ctx, cast=True)
        else:
            w_o_b = bf16_layer(w_o)
        h = _mm_res(o, w_o_b, 0, h, g1, seq)

        if i % 2 == 0:
            w13, w2, router = ffn_w13, ffn_w2, None
        else:
            w13, w2, router = moe_w13, moe_w2_flat, moe_router[j]
        comb = combc = None
        if ctx_out:
            uc2 = _norm_mod(hc, norm2_w[i], csh2, csc2, n_ctx, router)
            if router is not None:
                uc2, combc = uc2
            actc, w_a, w_b = _mm_swiglu(uc2, w13, w13, j, combc, cast=True)
            hc, w2_b = _mm_res(actc, w2, j, hc, cg2, n_ctx, cast=True)
        else:
            half = w13.shape[-1] // 2
            w_a, w_b = bf16_layer(w13[..., :half]), bf16_layer(w13[..., half:])
            w2_b = bf16_layer(w2)
        u2 = _norm_mod(h, norm2_w[i], sh2, sc2, seq, router)
        if router is not None:
            u2, comb = u2
        h = _mm_res(_mm_swiglu(u2, w_a, w_b, 0, comb), w2_b, 0, h, g2, seq)
    return h.reshape(batch, seq, d)
```

```python
import functools

import jax
import jax.numpy as jnp
from jax import lax
from jax.experimental import pallas as pl
from jax.experimental.pallas import tpu as pltpu

F32 = jnp.float32
BF16 = jnp.bfloat16

GRID_W = 64
RET_CHUNK = 128
ROPE_BASE = 10000.0
EPS = 1e-6
NEG_INF = -1e30
LOG2E = 1.4426950408889634

V7X_VMEM_BYTES = 64 * 1024 * 1024
VMEM_LIMIT_BYTES = V7X_VMEM_BYTES * 7 // 8
LANES = 128
ADA_ROWS = 8


def _params(n_grid_dims):
    return pltpu.CompilerParams(
        dimension_semantics=("arbitrary",) * n_grid_dims,
        vmem_limit_bytes=VMEM_LIMIT_BYTES,
    )


def _tile(n, pref):
    t = min(pref, n)
    while n % t:
        t //= 2
    return t


def _silu(x):
    return x * (1.0 / (1.0 + jnp.exp(-x)))


def _ada_kernel(x_ref, w_ref, b_ref, o_ref):
    xa = _silu(x_ref[...]).astype(BF16)
    w = w_ref[...].astype(BF16)
    o_ref[...] = jnp.dot(xa, w, preferred_element_type=F32) + b_ref[...]


def _ada_params(cond, ada_w, ada_b):
    depth, d, n = ada_w.shape
    bn = _tile(n, 512)
    return pl.pallas_call(
        _ada_kernel,
        out_shape=jax.ShapeDtypeStruct((depth, ADA_ROWS, n), F32),
        grid=(depth, n // bn),
        in_specs=[
            pl.BlockSpec((ADA_ROWS, d), lambda l, j: (0, 0)),
            pl.BlockSpec((None, d, bn), lambda l, j: (l, 0, j)),
            pl.BlockSpec((None, 1, bn), lambda l, j: (l, 0, j)),
        ],
        out_specs=pl.BlockSpec((None, ADA_ROWS, bn), lambda l, j: (l, 0, j)),
        compiler_params=_params(2),
        name="ada_params",
    )(cond, ada_w, ada_b.reshape(depth, 1, n))


def _norm_mod_kernel(x_ref, nw_ref, sh_ref, sc_ref, *rest, n_experts):
    x = x_ref[...]
    y = x * lax.rsqrt(jnp.mean(x * x, axis=-1, keepdims=True) + EPS)
    u = (y * nw_ref[...]) * (1.0 + sc_ref[...]) + sh_ref[...]
    ub = u.astype(BF16)
    if n_experts == 0:
        (u_ref,) = rest
        u_ref[...] = ub
        return
    wr_ref, u_ref, comb_ref = rest
    u_ref[...] = ub
    logits = jnp.dot(ub, wr_ref[...], preferred_element_type=F32)
    lane = lax.broadcasted_iota(jnp.int32, logits.shape, 1).astype(F32)
    valid = lane < n_experts
    logits = jnp.where(valid, logits, NEG_INF)
    e = jnp.exp(logits - jnp.max(logits, axis=-1, keepdims=True))
    probs = jnp.where(valid, e / jnp.sum(e, axis=-1, keepdims=True), -1.0)
    p1 = jnp.max(probs, axis=-1, keepdims=True)
    i1 = jnp.min(jnp.where(probs == p1, lane, float(LANES)), axis=-1, keepdims=True)
    rest_p = jnp.where(lane == i1, -1.0, probs)
    p2 = jnp.max(rest_p, axis=-1, keepdims=True)
    i2 = jnp.min(jnp.where(rest_p == p2, lane, float(LANES)), axis=-1, keepdims=True)
    inv = 1.0 / (p1 + p2)
    comb_ref[...] = jnp.where(lane == i1, p1 * inv, jnp.where(lane == i2, p2 * inv, 0.0))


def _norm_mod(x, norm_w, shift, scale, rows_per_batch, router_w=None):
    m, d = x.shape
    bm = _tile(rows_per_batch, 512)
    tiles_per_batch = rows_per_batch // bm
    mod_spec = pl.BlockSpec((None, 1, d), lambda i: (i // tiles_per_batch, 0, 0))
    in_specs = [
        pl.BlockSpec((bm, d), lambda i: (i, 0)),
        pl.BlockSpec((1, d), lambda i: (0, 0)),
        mod_spec,
        mod_spec,
    ]
    args = [x, norm_w.reshape(1, d), shift, scale]
    u_shape = jax.ShapeDtypeStruct((m, d), BF16)
    u_spec = pl.BlockSpec((bm, d), lambda i: (i, 0))
    if router_w is None:
        n_experts = 0
        out_shape, out_specs = u_shape, u_spec
    else:
        n_experts = router_w.shape[1]
        wr = jnp.zeros((d, LANES), BF16).at[:, :n_experts].set(router_w.astype(BF16))
        in_specs.append(pl.BlockSpec((d, LANES), lambda i: (0, 0)))
        args.append(wr)
        out_shape = (u_shape, jax.ShapeDtypeStruct((m, LANES), F32))
        out_specs = (u_spec, pl.BlockSpec((bm, LANES), lambda i: (i, 0)))
    return pl.pallas_call(
        functools.partial(_norm_mod_kernel, n_experts=n_experts),
        out_shape=out_shape,
        grid=(m // bm,),
        in_specs=in_specs,
        out_specs=out_specs,
        compiler_params=_params(1),
        name="norm_mod",
    )(*args)


def _weight_tile(w_ref, wout_ref):
    if wout_ref is None:
        return w_ref[...]
    w = w_ref[...].astype(BF16)
    wout_ref[...] = w
    return w


def _mm_na_qkv_kernel(x_ref, w_ref, qn_ref, kn_ref, o_ref, *wout, tiles_per_part, hd):
    j = pl.program_id(1)
    w = _weight_tile(w_ref, wout[0] if wout else None)
    acc = jnp.dot(x_ref[...], w, preferred_element_type=F32)
    n_heads = acc.shape[1] // hd

    is_q = j < tiles_per_part
    is_v = j >= 2 * tiles_per_part
    nw = jnp.where(is_q, qn_ref[...] * (hd ** -0.5 * LOG2E), kn_ref[...])
    for t in range(n_heads):
        a = acc[:, t * hd:(t + 1) * hd]
        y = a * lax.rsqrt(jnp.mean(a * a, axis=-1, keepdims=True) + EPS)
        o_ref[:, t * hd:(t + 1) * hd] = jnp.where(is_v, a, y * nw).astype(BF16)


def _cast_outputs(cast, out_shape, out_spec, w_shapes, w_specs):
    if not cast:
        return out_shape, out_spec
    return ((out_shape,) + tuple(jax.ShapeDtypeStruct(s, BF16) for s in w_shapes),
            (out_spec,) + tuple(w_specs))


def _mm_na_qkv(x, w, layer, qn_w, kn_w, cast=False):
    m, k = x.shape
    n = w.shape[2]
    hd = qn_w.shape[0]
    bm, bn = _tile(m, 1024), _tile(n // 3, 512 if cast else 1024)
    assert not cast or m == bm
    out_shape, out_specs = _cast_outputs(
        cast, jax.ShapeDtypeStruct((m, n), BF16), pl.BlockSpec((bm, bn), lambda i, j: (i, j)),
        [(1, k, n)], [pl.BlockSpec((None, k, bn), lambda i, j: (0, 0, j))])
    return pl.pallas_call(
        functools.partial(_mm_na_qkv_kernel, tiles_per_part=n // 3 // bn, hd=hd),
        out_shape=out_shape,
        grid=(m // bm, n // bn),
        in_specs=[
            pl.BlockSpec((bm, k), lambda i, j: (i, 0)),
            pl.BlockSpec((None, k, bn), lambda i, j: (layer, 0, j)),
            pl.BlockSpec((1, hd), lambda i, j: (0, 0)),
            pl.BlockSpec((1, hd), lambda i, j: (0, 0)),
        ],
        out_specs=out_specs,
        compiler_params=_params(2),
        name="mm_na_qkv",
    )(x, w, qn_w.reshape(1, hd), kn_w.reshape(1, hd))


def _mm_ret_proj_kernel(x_ref, w_ref, *rest, tiles_per_part, hd, rope, gate, cast):
    if rope:
        cos_ref, sin_ref = rest[:2]
        rest = rest[2:]
    o_ref = rest[0]
    j = pl.program_id(1)
    w = _weight_tile(w_ref, rest[1] if cast else None)
    acc = jnp.dot(x_ref[...], w, preferred_element_type=F32)
    if gate:
        o_ref[...] = _silu(acc).astype(BF16)
        return
    n_heads = acc.shape[1] // hd
    is_qk = j < 2 * tiles_per_part
    qk_scale = jnp.where(j >= tiles_per_part, hd ** -0.5, 1.0)
    for t in range(n_heads):
        a = acc[:, t * hd:(t + 1) * hd]
        qk = a
        if rope:
            lane = lax.broadcasted_iota(jnp.int32, a.shape, 1)
            first = (lane % (hd // 2)) < (hd // 4)
            partner = jnp.where(first, pltpu.roll(a, hd - hd // 4, 1), pltpu.roll(a, hd // 4, 1))
            qk = a * cos_ref[...] + partner * sin_ref[...]
        o_ref[:, t * hd:(t + 1) * hd] = jnp.where(is_qk, qk * qk_scale, a).astype(BF16)


def _mm_ret_proj(x, w, layer, col0, n, hd, rope_tables, seq, gate, cast=False):
    m, k = x.shape
    part = n if gate else n // 3
    bm, bn = _tile(seq, 1024), _tile(part, 512 if cast else 1024)
    assert not cast or m == bm
    assert col0 % bn == 0
    j0 = col0 // bn
    rope = rope_tables is not None and not gate
    in_specs = [
        pl.BlockSpec((bm, k), lambda i, j: (i, 0)),
        pl.BlockSpec((None, k, bn), lambda i, j: (layer, 0, j0 + j)),
    ]
    args = [x, w]
    if rope:
        tiles_per_seq = seq // bm
        tab_spec = pl.BlockSpec((bm, hd), lambda i, j: (i % tiles_per_seq, 0))
        in_specs += [tab_spec, tab_spec]
        args += list(rope_tables)
    out_shape, out_specs = _cast_outputs(
        cast, jax.ShapeDtypeStruct((m, n), BF16), pl.BlockSpec((bm, bn), lambda i, j: (i, j)),
        [(1, k, n)], [pl.BlockSpec((None, k, bn), lambda i, j: (0, 0, j))])
    return pl.pallas_call(
        functools.partial(_mm_ret_proj_kernel, tiles_per_part=part // bn, hd=hd, rope=rope, gate=gate,
                          cast=cast),
        out_shape=out_shape,
        grid=(m // bm, n // bn),
        in_specs=in_specs,
        out_specs=out_specs,
        compiler_params=_params(2),
        name="mm_ret_g" if gate else "mm_ret_qkv",
    )(*args)


def _mm_swiglu_kernel(x_ref, wa_ref, wb_ref, *rest, tiles_per_expert, cast):
    if tiles_per_expert:
        comb_ref = rest[0]
        rest = rest[1:]
    o_ref = rest[0]
    x = x_ref[...]
    a = jnp.dot(x, _weight_tile(wa_ref, rest[1] if cast else None), preferred_element_type=F32)
    b = jnp.dot(x, _weight_tile(wb_ref, rest[2] if cast else None), preferred_element_type=F32)
    act = _silu(a) * b
    if tiles_per_expert:
        e = pl.program_id(1) // tiles_per_expert
        comb = comb_ref[...]
        lane = lax.broadcasted_iota(jnp.int32, comb.shape, 1)
        act = act * jnp.sum(jnp.where(lane == e, comb, 0.0), axis=-1, keepdims=True)
    o_ref[...] = act.astype(BF16)


def _mm_swiglu(x, wa, wb, layer, comb=None, cast=False):
    m, k = x.shape
    bm = _tile(m, 1024)
    assert not cast or m == bm
    moe = comb is not None
    f = wa.shape[-1] // 2 if cast else wa.shape[-1]
    bn = _tile(f, 256 if cast else 512)
    nt = f // bn
    b_off = nt if cast else 0
    if moe:
        n_e = wa.shape[1]
        wa_spec = pl.BlockSpec((None, None, k, bn), lambda i, j: (layer, j // nt, 0, j % nt))
        wb_spec = pl.BlockSpec((None, None, k, bn), lambda i, j: (layer, j // nt, 0, b_off + j % nt))
        w_shape = (1, n_e, k, f)
        wo_spec = pl.BlockSpec((None, None, k, bn), lambda i, j: (0, j // nt, 0, j % nt))
        n_out = n_e * f
    else:
        wa_spec = pl.BlockSpec((None, k, bn), lambda i, j: (layer, 0, j))
        wb_spec = pl.BlockSpec((None, k, bn), lambda i, j: (layer, 0, b_off + j))
        w_shape = (1, k, f)
        wo_spec = pl.BlockSpec((None, k, bn), lambda i, j: (0, 0, j))
        n_out = f
    in_specs = [pl.BlockSpec((bm, k), lambda i, j: (i, 0)), wa_spec, wb_spec]
    args = [x, wa, wb]
    if moe:
        in_specs.append(pl.BlockSpec((bm, LANES), lambda i, j: (i, 0)))
        args.append(comb)
    out_shape, out_specs = _cast_outputs(
        cast, jax.ShapeDtypeStruct((m, n_out), BF16), pl.BlockSpec((bm, bn), lambda i, j: (i, j)),
        [w_shape, w_shape], [wo_spec, wo_spec])
    return pl.pallas_call(
        functools.partial(_mm_swiglu_kernel, tiles_per_expert=nt if moe else 0, cast=cast),
        out_shape=out_shape,
        grid=(m // bm, n_out // bn),
        in_specs=in_specs,
        out_specs=out_specs,
        compiler_params=_params(2),
        name="mm_swiglu",
    )(*args)


def _mm_res_kernel(x_ref, w_ref, res_ref, gate_ref, o_ref, *wout):
    w = _weight_tile(w_ref, wout[0] if wout else None)
    acc = jnp.dot(x_ref[...], w, preferred_element_type=F32)
    o_ref[...] = res_ref[...] + gate_ref[...] * acc


def _mm_res(x, w, layer, res, gate, rows_per_batch, cast=False):
    m, k = x.shape
    n = w.shape[2]
    bm, bn = _tile(rows_per_batch, 1024), _tile(n, 512 if cast else 1024)
    assert not cast or m == bm
    tiles_per_batch = rows_per_batch // bm
    out_shape, out_specs = _cast_outputs(
        cast, jax.ShapeDtypeStruct((m, n), F32), pl.BlockSpec((bm, bn), lambda i, j: (i, j)),
        [(1, k, n)], [pl.BlockSpec((None, k, bn), lambda i, j: (0, 0, j))])
    return pl.pallas_call(
        _mm_res_kernel,
        out_shape=out_shape,
        grid=(m // bm, n // bn),
        in_specs=[
            pl.BlockSpec((bm, k), lambda i, j: (i, 0)),
            pl.BlockSpec((None, k, bn), lambda i, j: (layer, 0, j)),
            pl.BlockSpec((bm, bn), lambda i, j: (i, j)),
            pl.BlockSpec((None, 1, bn), lambda i, j: (i // tiles_per_batch, 0, j)),
        ],
        out_specs=out_specs,
        compiler_params=_params(2),
        name="mm_res",
    )(x, w, res, gate)


def _softmax_pv(scores, values):
    m = functools.reduce(jnp.maximum, [jnp.max(s, axis=-1, keepdims=True) for s in scores])
    out = functools.reduce(
        jnp.add,
        [jnp.dot(jnp.exp2(s - m).astype(BF16), v, preferred_element_type=F32)
         for s, v in zip(scores, values)])
    hd = out.shape[1] // 2
    return out[:, :hd] / out[:, hd:]


def _qk(q, k):
    return lax.dot_general(q, k, (((1,), (1,)), ((), ())), preferred_element_type=F32)


NA_Q_ROWS = 4
NA_KEY_ROWS = 12


def _na_block_geometry(t, rows, win_r):
    u0 = min(max(t * NA_Q_ROWS - win_r // 2, 0), rows - NA_KEY_ROWS)
    geo = []
    for a in range(NA_Q_ROWS):
        r = t * NA_Q_ROWS + a
        r0 = min(max(r - win_r // 2, 0), rows - win_r)
        geo.append([(r0 <= u0 + j < r0 + win_r, u0 + j - r + win_r - 1) for j in range(NA_KEY_ROWS)])
    return u0, geo


def _na_build_slabs(pairs_ref, slab_ref, rows, win_r):
    n_blk = rows // NA_Q_ROWS
    lane = lax.broadcasted_iota(jnp.int32, (GRID_W, 2 * GRID_W), 1)
    neg = jnp.full((GRID_W, 2 * GRID_W), NEG_INF, F32)
    for v, t in enumerate((0, 1, n_blk - 1)):
        _, geo = _na_block_geometry(t, rows, win_r)
        for a in range(NA_Q_ROWS):
            for jj in range(NA_KEY_ROWS // 2):
                (ok0, d0), (ok1, _) = geo[a][2 * jj], geo[a][2 * jj + 1]
                if ok0 and ok1:
                    tile = pairs_ref[d0 + 1]
                elif ok0:
                    tile = jnp.where(lane < GRID_W, pairs_ref[d0 + 1], neg)
                elif ok1:
                    tile = jnp.where(lane >= GRID_W, pairs_ref[d0 + 1], neg)
                else:
                    tile = neg
                slab_ref[v, a * GRID_W:(a + 1) * GRID_W, jj * 2 * GRID_W:(jj + 1) * 2 * GRID_W] = tile


def _na_attn_kernel(q_ref, k_ref, v_ref, qc_ref, kc_ref, vc_ref, pairs_ref, o_ref, oc_ref,
                    bias_ref, vaug_ref, vcaug_ref, *, rows, win_r):
    hd = v_ref.shape[1]

    @pl.when(pl.program_id(1) == 0)
    def _():
        _na_build_slabs(pairs_ref, bias_ref, rows, win_r)

    @pl.when((pl.program_id(0) == 0) & (pl.program_id(1) == 0))
    def _():
        vaug_ref[:, hd:] = jnp.ones((vaug_ref.shape[0], hd), BF16)
        vcaug_ref[:, hd:] = jnp.ones((vcaug_ref.shape[0], hd), BF16)

    vaug_ref[:, :hd] = v_ref[...]
    vcaug_ref[:, :hd] = vc_ref[...]
    kc = kc_ref[...]
    vc = vcaug_ref[...]
    n_blk = rows // NA_Q_ROWS
    n_q = NA_Q_ROWS * GRID_W
    n_k = NA_KEY_ROWS * GRID_W

    def query_block(t, carry):
        u0 = jnp.clip(t * NA_Q_ROWS - win_r // 2, 0, rows - NA_KEY_ROWS)
        slab = jnp.where(t == 0, 0, jnp.where(t == n_blk - 1, 2, 1))
        q_rows = pl.ds(pl.multiple_of(t * n_q, n_q), n_q)
        k_rows = pl.ds(pl.multiple_of(u0 * GRID_W, GRID_W), n_k)
        q = q_ref[q_rows, :]
        s_win = _qk(q, k_ref[k_rows, :]) + bias_ref[slab]
        o = _softmax_pv([s_win, _qk(q, kc)], [vaug_ref[k_rows, :], vc])
        o_ref[q_rows, :] = o.astype(BF16)
        return carry

    lax.fori_loop(0, n_blk, query_block, 0, unroll=4)
    oc_ref[...] = _softmax_pv([_qk(qc_ref[...], kc)], [vc]).astype(BF16)


def _na_bias_pairs(rpb):
    n_heads, n_dr, n_dc = rpb.shape
    win_c = (n_dc + 1) // 2
    cols = jnp.arange(GRID_W)
    c0 = jnp.clip(cols - win_c // 2, 0, GRID_W - win_c)
    col_in = (cols[None, :] >= c0[:, None]) & (cols[None, :] < c0[:, None] + win_c)
    dc = jnp.clip(cols[None, :] - cols[:, None] + win_c - 1, 0, 2 * win_c - 2)
    masked = jnp.where(col_in, rpb[:, :, dc].astype(F32) * LOG2E, NEG_INF)
    neg = jnp.full((n_heads, 1, GRID_W, GRID_W), NEG_INF, F32)
    ext = jnp.concatenate([neg, masked, neg], axis=1)
    return jnp.concatenate([ext[:, :-1], ext[:, 1:]], axis=-1)


def _na_attention(qkv, qkvc, rpb, batch, seq, ctx_len, hd):
    n_heads = rpb.shape[0]
    win_r = (rpb.shape[1] + 1) // 2
    rows = seq // GRID_W
    n_blk = rows // NA_Q_ROWS
    assert rows % NA_Q_ROWS == 0 and n_blk >= 3 and NA_Q_ROWS >= win_r // 2
    assert (n_blk - 2) * NA_Q_ROWS - win_r // 2 <= rows - NA_KEY_ROWS
    assert NA_KEY_ROWS >= win_r + NA_Q_ROWS - 1 and NA_KEY_ROWS % 2 == 0
    d = n_heads * hd
    pairs = _na_bias_pairs(rpb)

    def spec(n_rows, part):
        return pl.BlockSpec((n_rows, hd), lambda h, b: (b, part * n_heads + h))

    return pl.pallas_call(
        functools.partial(_na_attn_kernel, rows=rows, win_r=win_r),
        out_shape=(jax.ShapeDtypeStruct((batch * seq, d), BF16),
                   jax.ShapeDtypeStruct((batch * ctx_len, d), BF16)),
        grid=(n_heads, batch),
        in_specs=[spec(seq, 0), spec(seq, 1), spec(seq, 2),
                  spec(ctx_len, 0), spec(ctx_len, 1), spec(ctx_len, 2),
                  pl.BlockSpec((None,) + pairs.shape[1:], lambda h, b: (h, 0, 0, 0))],
        out_specs=(pl.BlockSpec((seq, hd), lambda h, b: (b, h)),
                   pl.BlockSpec((ctx_len, hd), lambda h, b: (b, h))),
        scratch_shapes=[pltpu.VMEM((3, NA_Q_ROWS * GRID_W, NA_KEY_ROWS * GRID_W), F32),
                        pltpu.VMEM((seq, 2 * hd), BF16),
                        pltpu.VMEM((ctx_len, 2 * hd), BF16)],
        compiler_params=_params(2),
        name="na_attention",
    )(qkv, qkv, qkv, qkvc, qkvc, qkvc, pairs)


def _kv_outer(k, v):
    return lax.dot_general(k, v, (((0,), (0,)), ((), ())), preferred_element_type=F32)


def _retention_kernel(lgf_ref, lgb_ref, q_ref, k_ref, v_ref, g_ref, gnw_ref, *rest,
                      n_chunks, ctx_len):
    if ctx_len:
        kc_ref, vc_ref, o_ref, acc_ref, sf_ref, sb_ref = rest
    else:
        o_ref, acc_ref, sf_ref, sb_ref = rest
    c_len = RET_CHUNK
    h = pl.program_id(0)
    lgf = jnp.full((1, 1), lgf_ref[h], F32)
    lgb = jnp.full((1, 1), lgb_ref[h], F32)

    row = lax.broadcasted_iota(jnp.int32, (c_len, c_len), 0)
    col = lax.broadcasted_iota(jnp.int32, (c_len, c_len), 1)
    diff = (row - col).astype(F32)
    intra = jnp.where(diff >= 0, jnp.exp(lgf * jnp.maximum(diff, 0.0)),
                      jnp.exp(lgb * jnp.maximum(-diff, 0.0)))
    pos = lax.broadcasted_iota(jnp.int32, (c_len, 1), 0).astype(F32)
    q_dec_f = jnp.exp(lgf * (pos + 1.0))
    k_dec_f = jnp.exp(lgf * (c_len - 1.0 - pos))
    chunk_dec_f = jnp.exp(lgf * c_len)
    q_dec_b = jnp.exp(lgb * (c_len - pos))
    k_dec_b = jnp.exp(lgb * pos)
    chunk_dec_b = jnp.exp(lgb * c_len)

    if ctx_len:
        cpos = lax.broadcasted_iota(jnp.int32, (ctx_len, 1), 0).astype(F32)
        kc = kc_ref[...].astype(F32)
        vc = vc_ref[...]
        sf_ref[...] = _kv_outer((kc * jnp.exp(lgf * (ctx_len - 1.0 - cpos))).astype(BF16), vc)
        sb_ref[...] = _kv_outer((kc * jnp.exp(lgb * cpos)).astype(BF16), vc)
    else:
        sf_ref[...] = jnp.zeros_like(sf_ref)
        sb_ref[...] = jnp.zeros_like(sb_ref)

    def chunk_slice(c):
        return pl.ds(pl.multiple_of(c * c_len, c_len), c_len)

    def fwd_part(sl):
        q = q_ref[sl, :]
        k = k_ref[sl, :]
        v = v_ref[sl, :]
        s = (_qk(q, k) * intra).astype(BF16)
        qf = (q.astype(F32) * q_dec_f).astype(BF16)
        val = (jnp.dot(s, v, preferred_element_type=F32)
               + jnp.dot(qf, sf_ref[...].astype(BF16), preferred_element_type=F32))
        kf = (k.astype(F32) * k_dec_f).astype(BF16)
        sf_ref[...] = sf_ref[...] * chunk_dec_f + _kv_outer(kf, v)
        return val

    def bwd_part(sl):
        q = q_ref[sl, :]
        k = k_ref[sl, :]
        v = v_ref[sl, :]
        qb = (q.astype(F32) * q_dec_b).astype(BF16)
        val = jnp.dot(qb, sb_ref[...].astype(BF16), preferred_element_type=F32)
        kb = (k.astype(F32) * k_dec_b).astype(BF16)
        sb_ref[...] = sb_ref[...] * chunk_dec_b + _kv_outer(kb, v)
        return val

    def finish(sl, o):
        mu = jnp.mean(o, axis=-1, keepdims=True)
        cen = o - mu
        var = jnp.mean(cen * cen, axis=-1, keepdims=True)
        normed = cen * lax.rsqrt(var + EPS)
        o_ref[sl, :] = (normed * gnw_ref[...] * g_ref[sl, :].astype(F32)).astype(BF16)

    half = n_chunks // 2

    def first_half(t, carry):
        lo, hi = chunk_slice(t), chunk_slice(n_chunks - 1 - t)
        acc_ref[lo, :] = fwd_part(lo)
        acc_ref[hi, :] = bwd_part(hi)
        return carry

    def second_half(t, carry):
        hi, lo = chunk_slice(half + t), chunk_slice(half - 1 - t)
        finish(hi, acc_ref[hi, :] + fwd_part(hi))
        finish(lo, acc_ref[lo, :] + bwd_part(lo))
        return carry

    unroll = 2 if half % 2 == 0 else 1
    lax.fori_loop(0, half, first_half, 0, unroll=unroll)
    lax.fori_loop(0, half, second_half, 0, unroll=unroll)


def _retention(qkv, gate, lg_f, lg_b, gn_w, batch, n_tok, hd, qkv_ctx=None, ctx_len=0):
    n_heads = lg_f.shape[0]
    d = n_heads * hd
    n_chunks = n_tok // RET_CHUNK
    assert n_tok % (2 * RET_CHUNK) == 0

    def spec(n_rows, part):
        return pl.BlockSpec((n_rows, hd), lambda h, b: (b, part * n_heads + h))

    smem = pl.BlockSpec(memory_space=pltpu.SMEM)
    in_specs = [smem, smem, spec(n_tok, 0), spec(n_tok, 1), spec(n_tok, 2), spec(n_tok, 0),
                pl.BlockSpec((1, hd), lambda h, b: (0, h))]
    args = [lg_f, lg_b, qkv, qkv, qkv, gate, gn_w.reshape(1, d)]
    if ctx_len:
        in_specs += [spec(ctx_len, 1), spec(ctx_len, 2)]
        args += [qkv_ctx, qkv_ctx]
    return pl.pallas_call(
        functools.partial(_retention_kernel, n_chunks=n_chunks, ctx_len=ctx_len),
        out_shape=jax.ShapeDtypeStruct((batch * n_tok, d), BF16),
        grid=(n_heads, batch),
        in_specs=in_specs,
        out_specs=pl.BlockSpec((n_tok, hd), lambda h, b: (b, h)),
        scratch_shapes=[pltpu.VMEM((n_tok, hd), F32),
                        pltpu.VMEM((hd, hd), F32),
                        pltpu.VMEM((hd, hd), F32)],
        compiler_params=_params(2),
        name="retention",
    )(*args)


def _rope_tables(seq, hd):
    t = jnp.arange(seq)
    row = (t // GRID_W).astype(F32)
    col = (t % GRID_W).astype(F32)
    axis_dim = hd // 2
    inv_freq = jnp.power(ROPE_BASE, -jnp.arange(0, axis_dim, 2, dtype=F32) / axis_dim)
    ang_r = row[:, None] * inv_freq[None, :]
    ang_c = col[:, None] * inv_freq[None, :]
    cos = jnp.concatenate([jnp.cos(ang_r), jnp.cos(ang_r), jnp.cos(ang_c), jnp.cos(ang_c)], axis=-1)
    sin = jnp.concatenate([-jnp.sin(ang_r), jnp.sin(ang_r), -jnp.sin(ang_c), jnp.sin(ang_c)], axis=-1)
    return cos, sin


def kernel(x, c, ctx, c_ctx, ada_w, ada_b, norm1_w, norm2_w, na_wqkv, na_wo, na_qnorm_w, na_knorm_w, na_rpb, ret_wqkvg, ret_wo, ret_decay_f, ret_decay_b, ret_gn_w, ffn_w13, ffn_w2, moe_router, moe_w13, moe_w2):
    batch, seq, d = x.shape
    ctx_len = ctx.shape[1]
    n_ctx = batch * ctx_len
    depth = ada_w.shape[0]
    na_hd = na_qnorm_w.shape[1]
    ret_heads = ret_decay_f.shape[1]
    ret_hd = d // ret_heads
    assert batch + 1 <= ADA_ROWS

    cond = jnp.zeros((ADA_ROWS, d), F32).at[:batch].set(c).at[batch].set(c_ctx)
    ada = _ada_params(cond, ada_w, ada_b)
    rope = _rope_tables(seq, ret_hd)
    moe_w2_flat = moe_w2.reshape(moe_w2.shape[0], -1, d)

    h = x.reshape(batch * seq, d)
    hc = ctx.reshape(n_ctx, d)
    for i in range(depth):
        j = i // 2
        ctx_out = i < depth - 1
        mods = ada[i].reshape(ADA_ROWS, 6, 1, d)
        sh1, sc1, g1, sh2, sc2, g2 = (mods[:batch, p] for p in range(6))
        csh1, csc1, cg1, csh2, csc2, cg2 = (mods[batch:batch + 1, p] for p in range(6))

        def bf16_layer(w):
            return w[j:j + 1].astype(BF16)

        u = _norm_mod(h, norm1_w[i], sh1, sc1, seq)
        uc = _norm_mod(hc, norm1_w[i], csh1, csc1, n_ctx)
        if i % 2 == 0:
            w_o = na_wo
            qkvc, w_qkv = _mm_na_qkv(uc, na_wqkv, j, na_qnorm_w[j], na_knorm_w[j], cast=True)
            qkv = _mm_na_qkv(u, w_qkv, 0, na_qnorm_w[j], na_knorm_w[j])
            o, oc = _na_attention(qkv, qkvc, na_rpb[j], batch, seq, ctx_len, na_hd)
        else:
            w_o = ret_wo
            lg_f = jax.nn.log_sigmoid(ret_decay_f[j].astype(F32))
            lg_b = jax.nn.log_sigmoid(ret_decay_b[j].astype(F32))
            qkvc, w_qkv = _mm_ret_proj(uc, ret_wqkvg, j, 0, 3 * d, ret_hd, None, n_ctx, False, cast=True)
            gc, w_g = _mm_ret_proj(uc, ret_wqkvg, j, 3 * d, d, ret_hd, None, n_ctx, True, cast=True)
            qkv = _mm_ret_proj(u, w_qkv, 0, 0, 3 * d, ret_hd, rope, seq, False)
            g = _mm_ret_proj(u, w_g, 0, 0, d, ret_hd, None, seq, True)
            o = _retention(qkv, g, lg_f, lg_b, ret_gn_w[j], batch, seq, ret_hd, qkvc, ctx_len)
            if ctx_out:
                oc = _retention(qkvc, gc, lg_f, lg_b, ret_gn_w[j], batch, ctx_len, ret_hd)
        if ctx_out:
            hc, w_o_b = _mm_res(oc, w_o, j, hc, cg1, n_ctx, cast=True)
        else:
            w_o_b = bf16_layer(w_o)
        h = _mm_res(o, w_o_b, 0, h, g1, seq)

        if i % 2 == 0:
            w13, w2, router = ffn_w13, ffn_w2, None
        else:
            w13, w2, router = moe_w13, moe_w2_flat, moe_router[j]
        comb = combc = None
        if ctx_out:
            uc2 = _norm_mod(hc, norm2_w[i], csh2, csc2, n_ctx, router)
            if router is not None:
                uc2, combc = uc2
            actc, w_a, w_b = _mm_swiglu(uc2, w13, w13, j, combc, cast=True)
            hc, w2_b = _mm_res(actc, w2, j, hc, cg2, n_ctx, cast=True)
        else:
            half = w13.shape[-1] // 2
            w_a, w_b = bf16_layer(w13[..., :half]), bf16_layer(w13[..., half:])
            w2_b = bf16_layer(w2)
        u2 = _norm_mod(h, norm2_w[i], sh2, sc2, seq, router)
        if router is not None:
            u2, comb = u2
        h = _mm_res(_mm_swiglu(u2, w_a, w_b, 0, comb), w2_b, 0, h, g2, seq)
    return h.reshape(batch, seq, d)
```

```python
import functools

import jax
import jax.numpy as jnp
from jax import lax
from jax.experimental import pallas as pl
from jax.experimental.pallas import tpu as pltpu

F32 = jnp.float32
BF16 = jnp.bfloat16

GRID_W = 64
RET_CHUNK = 128
ROPE_BASE = 10000.0
EPS = 1e-6
NEG_INF = -1e30
LOG2E = 1.4426950408889634

V7X_VMEM_BYTES = 64 * 1024 * 1024
VMEM_LIMIT_BYTES = V7X_VMEM_BYTES * 7 // 8
LANES = 128
ADA_ROWS = 8


def _params(n_grid_dims):
    return pltpu.CompilerParams(
        dimension_semantics=("arbitrary",) * n_grid_dims,
        vmem_limit_bytes=VMEM_LIMIT_BYTES,
    )


def _tile(n, pref):
    t = min(pref, n)
    while n % t:
        t //= 2
    return t


def _silu(x):
    return x * (1.0 / (1.0 + jnp.exp(-x)))


def _ada_kernel(x_ref, w_ref, b_ref, o_ref):
    xa = _silu(x_ref[...]).astype(BF16)
    w = w_ref[...].astype(BF16)
    o_ref[...] = jnp.dot(xa, w, preferred_element_type=F32) + b_ref[...]


def _ada_params(cond, ada_w, ada_b):
    depth, d, n = ada_w.shape
    bn = _tile(n, 512)
    return pl.pallas_call(
        _ada_kernel,
        out_shape=jax.ShapeDtypeStruct((depth, ADA_ROWS, n), F32),
        grid=(depth, n // bn),
        in_specs=[
            pl.BlockSpec((ADA_ROWS, d), lambda l, j: (0, 0)),
            pl.BlockSpec((None, d, bn), lambda l, j: (l, 0, j)),
            pl.BlockSpec((None, 1, bn), lambda l, j: (l, 0, j)),
        ],
        out_specs=pl.BlockSpec((None, ADA_ROWS, bn), lambda l, j: (l, 0, j)),
        compiler_params=_params(2),
        name="ada_params",
    )(cond, ada_w, ada_b.reshape(depth, 1, n))


def _norm_mod_kernel(x_ref, nw_ref, sh_ref, sc_ref, *rest, n_experts):
    x = x_ref[...]
    y = x * lax.rsqrt(jnp.mean(x * x, axis=-1, keepdims=True) + EPS)
    u = (y * nw_ref[...]) * (1.0 + sc_ref[...]) + sh_ref[...]
    ub = u.astype(BF16)
    if n_experts == 0:
        (u_ref,) = rest
        u_ref[...] = ub
        return
    wr_ref, u_ref, comb_ref = rest
    u_ref[...] = ub
    logits = jnp.dot(ub, wr_ref[...], preferred_element_type=F32)
    lane = lax.broadcasted_iota(jnp.int32, logits.shape, 1).astype(F32)
    valid = lane < n_experts
    logits = jnp.where(valid, logits, NEG_INF)
    e = jnp.exp(logits - jnp.max(logits, axis=-1, keepdims=True))
    probs = jnp.where(valid, e / jnp.sum(e, axis=-1, keepdims=True), -1.0)
    p1 = jnp.max(probs, axis=-1, keepdims=True)
    i1 = jnp.min(jnp.where(probs == p1, lane, float(LANES)), axis=-1, keepdims=True)
    rest_p = jnp.where(lane == i1, -1.0, probs)
    p2 = jnp.max(rest_p, axis=-1, keepdims=True)
    i2 = jnp.min(jnp.where(rest_p == p2, lane, float(LANES)), axis=-1, keepdims=True)
    inv = 1.0 / (p1 + p2)
    comb_ref[...] = jnp.where(lane == i1, p1 * inv, jnp.where(lane == i2, p2 * inv, 0.0))


def _norm_mod(x, norm_w, shift, scale, rows_per_batch, router_w=None):
    m, d = x.shape
    bm = _tile(rows_per_batch, 512)
    tiles_per_batch = rows_per_batch // bm
    mod_spec = pl.BlockSpec((None, 1, d), lambda i: (i // tiles_per_batch, 0, 0))
    in_specs = [
        pl.BlockSpec((bm, d), lambda i: (i, 0)),
        pl.BlockSpec((1, d), lambda i: (0, 0)),
        mod_spec,
        mod_spec,
    ]
    args = [x, norm_w.reshape(1, d), shift, scale]
    u_shape = jax.ShapeDtypeStruct((m, d), BF16)
    u_spec = pl.BlockSpec((bm, d), lambda i: (i, 0))
    if router_w is None:
        n_experts = 0
        out_shape, out_specs = u_shape, u_spec
    else:
        n_experts = router_w.shape[1]
        wr = jnp.zeros((d, LANES), BF16).at[:, :n_experts].set(router_w.astype(BF16))
        in_specs.append(pl.BlockSpec((d, LANES), lambda i: (0, 0)))
        args.append(wr)
        out_shape = (u_shape, jax.ShapeDtypeStruct((m, LANES), F32))
        out_specs = (u_spec, pl.BlockSpec((bm, LANES), lambda i: (i, 0)))
    return pl.pallas_call(
        functools.partial(_norm_mod_kernel, n_experts=n_experts),
        out_shape=out_shape,
        grid=(m // bm,),
        in_specs=in_specs,
        out_specs=out_specs,
        compiler_params=_params(1),
        name="norm_mod",
    )(*args)


def _weight_tile(w_ref, wout_ref):
    if wout_ref is None:
        return w_ref[...]
    w = w_ref[...].astype(BF16)
    wout_ref[...] = w
    return w


def _mm_na_qkv_kernel(x_ref, w_ref, qn_ref, kn_ref, o_ref, *wout, tiles_per_part, hd):
    j = pl.program_id(1)
    w = _weight_tile(w_ref, wout[0] if wout else None)
    acc = jnp.dot(x_ref[...], w, preferred_element_type=F32)
    n_heads = acc.shape[1] // hd

    is_q = j < tiles_per_part
    is_v = j >= 2 * tiles_per_part
    nw = jnp.where(is_q, qn_ref[...] * (hd ** -0.5 * LOG2E), kn_ref[...])
    for t in range(n_heads):
        a = acc[:, t * hd:(t + 1) * hd]
        y = a * lax.rsqrt(jnp.mean(a * a, axis=-1, keepdims=True) + EPS)
        o_ref[:, t * hd:(t + 1) * hd] = jnp.where(is_v, a, y * nw).astype(BF16)


def _cast_outputs(cast, out_shape, out_spec, w_shapes, w_specs):
    if not cast:
        return out_shape, out_spec
    return ((out_shape,) + tuple(jax.ShapeDtypeStruct(s, BF16) for s in w_shapes),
            (out_spec,) + tuple(w_specs))


def _mm_na_qkv(x, w, layer, qn_w, kn_w, cast=False):
    m, k = x.shape
    n = w.shape[2]
    hd = qn_w.shape[0]
    bm, bn = _tile(m, 1024), _tile(n // 3, 512 if cast else 1024)
    assert not cast or m == bm
    out_shape, out_specs = _cast_outputs(
        cast, jax.ShapeDtypeStruct((m, n), BF16), pl.BlockSpec((bm, bn), lambda i, j: (i, j)),
        [(1, k, n)], [pl.BlockSpec((None, k, bn), lambda i, j: (0, 0, j))])
    return pl.pallas_call(
        functools.partial(_mm_na_qkv_kernel, tiles_per_part=n // 3 // bn, hd=hd),
        out_shape=out_shape,
        grid=(m // bm, n // bn),
        in_specs=[
            pl.BlockSpec((bm, k), lambda i, j: (i, 0)),
            pl.BlockSpec((None, k, bn), lambda i, j: (layer, 0, j)),
            pl.BlockSpec((1, hd), lambda i, j: (0, 0)),
            pl.BlockSpec((1, hd), lambda i, j: (0, 0)),
        ],
        out_specs=out_specs,
        compiler_params=_params(2),
        name="mm_na_qkv",
    )(x, w, qn_w.reshape(1, hd), kn_w.reshape(1, hd))


def _mm_ret_proj_kernel(x_ref, w_ref, *rest, tiles_per_part, hd, rope, gate, cast):
    if rope:
        cos_ref, sin_ref = rest[:2]
        rest = rest[2:]
    o_ref = rest[0]
    j = pl.program_id(1)
    w = _weight_tile(w_ref, rest[1] if cast else None)
    acc = jnp.dot(x_ref[...], w, preferred_element_type=F32)
    if gate:
        o_ref[...] = _silu(acc).astype(BF16)
        return
    n_heads = acc.shape[1] // hd
    is_qk = j < 2 * tiles_per_part
    qk_scale = jnp.where(j >= tiles_per_part, hd ** -0.5, 1.0)
    for t in range(n_heads):
        a = acc[:, t * hd:(t + 1) * hd]
        qk = a
        if rope:
            lane = lax.broadcasted_iota(jnp.int32, a.shape, 1)
            first = (lane % (hd // 2)) < (hd // 4)
            partner = jnp.where(first, pltpu.roll(a, hd - hd // 4, 1), pltpu.roll(a, hd // 4, 1))
            qk = a * cos_ref[...] + partner * sin_ref[...]
        o_ref[:, t * hd:(t + 1) * hd] = jnp.where(is_qk, qk * qk_scale, a).astype(BF16)


def _mm_ret_proj(x, w, layer, col0, n, hd, rope_tables, seq, gate, cast=False):
    m, k = x.shape
    part = n if gate else n // 3
    bm, bn = _tile(seq, 1024), _tile(part, 512 if cast else 1024)
    assert not cast or m == bm
    assert col0 % bn == 0
    j0 = col0 // bn
    rope = rope_tables is not None and not gate
    in_specs = [
        pl.BlockSpec((bm, k), lambda i, j: (i, 0)),
        pl.BlockSpec((None, k, bn), lambda i, j: (layer, 0, j0 + j)),
    ]
    args = [x, w]
    if rope:
        tiles_per_seq = seq // bm
        tab_spec = pl.BlockSpec((bm, hd), lambda i, j: (i % tiles_per_seq, 0))
        in_specs += [tab_spec, tab_spec]
        args += list(rope_tables)
    out_shape, out_specs = _cast_outputs(
        cast, jax.ShapeDtypeStruct((m, n), BF16), pl.BlockSpec((bm, bn), lambda i, j: (i, j)),
        [(1, k, n)], [pl.BlockSpec((None, k, bn), lambda i, j: (0, 0, j))])
    return pl.pallas_call(
        functools.partial(_mm_ret_proj_kernel, tiles_per_part=part // bn, hd=hd, rope=rope, gate=gate,
                          cast=cast),
        out_shape=out_shape,
        grid=(m // bm, n // bn),
        in_specs=in_specs,
        out_specs=out_specs,
        compiler_params=_params(2),
        name="mm_ret_g" if gate else "mm_ret_qkv",
    )(*args)


def _mm_swiglu_kernel(x_ref, wa_ref, wb_ref, *rest, tiles_per_expert, cast):
    if tiles_per_expert:
        comb_ref = rest[0]
        rest = rest[1:]
    o_ref = rest[0]
    x = x_ref[...]
    a = jnp.dot(x, _weight_tile(wa_ref, rest[1] if cast else None), preferred_element_type=F32)
    b = jnp.dot(x, _weight_tile(wb_ref, rest[2] if cast else None), preferred_element_type=F32)
    act = _silu(a) * b
    if tiles_per_expert:
        e = pl.program_id(1) // tiles_per_expert
        comb = comb_ref[...]
        lane = lax.broadcasted_iota(jnp.int32, comb.shape, 1)
        act = act * jnp.sum(jnp.where(lane == e, comb, 0.0), axis=-1, keepdims=True)
    o_ref[...] = act.astype(BF16)


def _mm_swiglu(x, wa, wb, layer, comb=None, cast=False):
    m, k = x.shape
    bm = _tile(m, 1024)
    assert not cast or m == bm
    moe = comb is not None
    f = wa.shape[-1] // 2 if cast else wa.shape[-1]
    bn = _tile(f, 256 if cast else 512)
    nt = f // bn
    b_off = nt if cast else 0
    if moe:
        n_e = wa.shape[1]
        wa_spec = pl.BlockSpec((None, None, k, bn), lambda i, j: (layer, j // nt, 0, j % nt))
        wb_spec = pl.BlockSpec((None, None, k, bn), lambda i, j: (layer, j // nt, 0, b_off + j % nt))
        w_shape = (1, n_e, k, f)
        wo_spec = pl.BlockSpec((None, None, k, bn), lambda i, j: (0, j // nt, 0, j % nt))
        n_out = n_e * f
    else:
        wa_spec = pl.BlockSpec((None, k, bn), lambda i, j: (layer, 0, j))
        wb_spec = pl.BlockSpec((None, k, bn), lambda i, j: (layer, 0, b_off + j))
        w_shape = (1, k, f)
        wo_spec = pl.BlockSpec((None, k, bn), lambda i, j: (0, 0, j))
        n_out = f
    in_specs = [pl.BlockSpec((bm, k), lambda i, j: (i, 0)), wa_spec, wb_spec]
    args = [x, wa, wb]
    if moe:
        in_specs.append(pl.BlockSpec((bm, LANES), lambda i, j: (i, 0)))
        args.append(comb)
    out_shape, out_specs = _cast_outputs(
        cast, jax.ShapeDtypeStruct((m, n_out), BF16), pl.BlockSpec((bm, bn), lambda i, j: (i, j)),
        [w_shape, w_shape], [wo_spec, wo_spec])
    return pl.pallas_call(
        functools.partial(_mm_swiglu_kernel, tiles_per_expert=nt if moe else 0, cast=cast),
        out_shape=out_shape,
        grid=(m // bm, n_out // bn),
        in_specs=in_specs,
        out_specs=out_specs,
        compiler_params=_params(2),
        name="mm_swiglu",
    )(*args)


def _mm_res_kernel(x_ref, w_ref, res_ref, gate_ref, o_ref, *wout):
    w = _weight_tile(w_ref, wout[0] if wout else None)
    acc = jnp.dot(x_ref[...], w, preferred_element_type=F32)
    o_ref[...] = res_ref[...] + gate_ref[...] * acc


def _mm_res(x, w, layer, res, gate, rows_per_batch, cast=False):
    m, k = x.shape
    n = w.shape[2]
    bm, bn = _tile(rows_per_batch, 1024), _tile(n, 512 if cast else 1024)
    assert not cast or m == bm
    tiles_per_batch = rows_per_batch // bm
    out_shape, out_specs = _cast_outputs(
        cast, jax.ShapeDtypeStruct((m, n), F32), pl.BlockSpec((bm, bn), lambda i, j: (i, j)),
        [(1, k, n)], [pl.BlockSpec((None, k, bn), lambda i, j: (0, 0, j))])
    return pl.pallas_call(
        _mm_res_kernel,
        out_shape=out_shape,
        grid=(m // bm, n // bn),
        in_specs=[
            pl.BlockSpec((bm, k), lambda i, j: (i, 0)),
            pl.BlockSpec((None, k, bn), lambda i, j: (layer, 0, j)),
            pl.BlockSpec((bm, bn), lambda i, j: (i, j)),
            pl.BlockSpec((None, 1, bn), lambda i, j: (i // tiles_per_batch, 0, j)),
        ],
        out_specs=out_specs,
        compiler_params=_params(2),
        name="mm_res",
    )(x, w, res, gate)


def _softmax_pv(scores, values):
    m = functools.reduce(jnp.maximum, [jnp.max(s, axis=-1, keepdims=True) for s in scores])
    out = functools.reduce(
        jnp.add,
        [jnp.dot(jnp.exp2(s - m).astype(BF16), v, preferred_element_type=F32)
         for s, v in zip(scores, values)])
    hd = out.shape[1] // 2
    return out[:, :hd] / out[:, hd:]


def _qk(q, k):
    return lax.dot_general(q, k, (((1,), (1,)), ((), ())), preferred_element_type=F32)


NA_Q_ROWS = 4
NA_KEY_ROWS = 12


def _na_block_geometry(t, rows, win_r):
    u0 = min(max(t * NA_Q_ROWS - win_r // 2, 0), rows - NA_KEY_ROWS)
    geo = []
    for a in range(NA_Q_ROWS):
        r = t * NA_Q_ROWS + a
        r0 = min(max(r - win_r // 2, 0), rows - win_r)
        geo.append([(r0 <= u0 + j < r0 + win_r, u0 + j - r + win_r - 1) for j in range(NA_KEY_ROWS)])
    return u0, geo


def _na_build_slabs(pairs_ref, slab_ref, rows, win_r):
    n_blk = rows // NA_Q_ROWS
    lane = lax.broadcasted_iota(jnp.int32, (GRID_W, 2 * GRID_W), 1)
    neg = jnp.full((GRID_W, 2 * GRID_W), NEG_INF, F32)
    for v, t in enumerate((0, 1, n_blk - 1)):
        _, geo = _na_block_geometry(t, rows, win_r)
        for a in range(NA_Q_ROWS):
            for jj in range(NA_KEY_ROWS // 2):
                (ok0, d0), (ok1, _) = geo[a][2 * jj], geo[a][2 * jj + 1]
                if ok0 and ok1:
                    tile = pairs_ref[d0 + 1]
                elif ok0:
                    tile = jnp.where(lane < GRID_W, pairs_ref[d0 + 1], neg)
                elif ok1:
                    tile = jnp.where(lane >= GRID_W, pairs_ref[d0 + 1], neg)
                else:
                    tile = neg
                slab_ref[v, a * GRID_W:(a + 1) * GRID_W, jj * 2 * GRID_W:(jj + 1) * 2 * GRID_W] = tile


def _na_attn_kernel(q_ref, k_ref, v_ref, qc_ref, kc_ref, vc_ref, pairs_ref, o_ref, oc_ref,
                    bias_ref, vaug_ref, vcaug_ref, *, rows, win_r):
    hd = v_ref.shape[1]

    @pl.when(pl.program_id(1) == 0)
    def _():
        _na_build_slabs(pairs_ref, bias_ref, rows, win_r)

    @pl.when((pl.program_id(0) == 0) & (pl.program_id(1) == 0))
    def _():
        vaug_ref[:, hd:] = jnp.ones((vaug_ref.shape[0], hd), BF16)
        vcaug_ref[:, hd:] = jnp.ones((vcaug_ref.shape[0], hd), BF16)

    vaug_ref[:, :hd] = v_ref[...]
    vcaug_ref[:, :hd] = vc_ref[...]
    kc = kc_ref[...]
    vc = vcaug_ref[...]
    n_blk = rows // NA_Q_ROWS
    n_q = NA_Q_ROWS * GRID_W
    n_k = NA_KEY_ROWS * GRID_W

    def query_block(t, carry):
        u0 = jnp.clip(t * NA_Q_ROWS - win_r // 2, 0, rows - NA_KEY_ROWS)
        slab = jnp.where(t == 0, 0, jnp.where(t == n_blk - 1, 2, 1))
        q_rows = pl.ds(pl.multiple_of(t * n_q, n_q), n_q)
        k_rows = pl.ds(pl.multiple_of(u0 * GRID_W, GRID_W), n_k)
        q = q_ref[q_rows, :]
        s_win = _qk(q, k_ref[k_rows, :]) + bias_ref[slab]
        o = _softmax_pv([s_win, _qk(q, kc)], [vaug_ref[k_rows, :], vc])
        o_ref[q_rows, :] = o.astype(BF16)
        return carry

    lax.fori_loop(0, n_blk, query_block, 0, unroll=4)
    oc_ref[...] = _softmax_pv([_qk(qc_ref[...], kc)], [vc]).astype(BF16)


def _na_bias_pairs(rpb):
    n_heads, n_dr, n_dc = rpb.shape
    win_c = (n_dc + 1) // 2
    cols = jnp.arange(GRID_W)
    c0 = jnp.clip(cols - win_c // 2, 0, GRID_W - win_c)
    col_in = (cols[None, :] >= c0[:, None]) & (cols[None, :] < c0[:, None] + win_c)
    dc = jnp.clip(cols[None, :] - cols[:, None] + win_c - 1, 0, 2 * win_c - 2)
    masked = jnp.where(col_in, rpb[:, :, dc].astype(F32) * LOG2E, NEG_INF)
    neg = jnp.full((n_heads, 1, GRID_W, GRID_W), NEG_INF, F32)
    ext = jnp.concatenate([neg, masked, neg], axis=1)
    return jnp.concatenate([ext[:, :-1], ext[:, 1:]], axis=-1)


def _na_attention(qkv, qkvc, rpb, batch, seq, ctx_len, hd):
    n_heads = rpb.shape[0]
    win_r = (rpb.shape[1] + 1) // 2
    rows = seq // GRID_W
    n_blk = rows // NA_Q_ROWS
    assert rows % NA_Q_ROWS == 0 and n_blk >= 3 and NA_Q_ROWS >= win_r // 2
    assert (n_blk - 2) * NA_Q_ROWS - win_r // 2 <= rows - NA_KEY_ROWS
    assert NA_KEY_ROWS >= win_r + NA_Q_ROWS - 1 and NA_KEY_ROWS % 2 == 0
    d = n_heads * hd
    pairs = _na_bias_pairs(rpb)

    def spec(n_rows, part):
        return pl.BlockSpec((n_rows, hd), lambda h, b: (b, part * n_heads + h))

    return pl.pallas_call(
        functools.partial(_na_attn_kernel, rows=rows, win_r=win_r),
        out_shape=(jax.ShapeDtypeStruct((batch * seq, d), BF16),
                   jax.ShapeDtypeStruct((batch * ctx_len, d), BF16)),
        grid=(n_heads, batch),
        in_specs=[spec(seq, 0), spec(seq, 1), spec(seq, 2),
                  spec(ctx_len, 0), spec(ctx_len, 1), spec(ctx_len, 2),
                  pl.BlockSpec((None,) + pairs.shape[1:], lambda h, b: (h, 0, 0, 0))],
        out_specs=(pl.BlockSpec((seq, hd), lambda h, b: (b, h)),
                   pl.BlockSpec((ctx_len, hd), lambda h, b: (b, h))),
        scratch_shapes=[pltpu.VMEM((3, NA_Q_ROWS * GRID_W, NA_KEY_ROWS * GRID_W), F32),
                        pltpu.VMEM((seq, 2 * hd), BF16),
                        pltpu.VMEM((ctx_len, 2 * hd), BF16)],
        compiler_params=_params(2),
        name="na_attention",
    )(qkv, qkv, qkv, qkvc, qkvc, qkvc, pairs)


def _kv_outer(k, v):
    return lax.dot_general(k, v, (((0,), (0,)), ((), ())), preferred_element_type=F32)


def _retention_kernel(lgf_ref, lgb_ref, q_ref, k_ref, v_ref, g_ref, gnw_ref, *rest,
                      c_len, n_chunks, ctx_len):
    if ctx_len:
        kc_ref, vc_ref, o_ref, acc_ref, sf_ref, sb_ref, kvf_ref, kvb_ref = rest
    else:
        o_ref, acc_ref, sf_ref, sb_ref, kvf_ref, kvb_ref = rest
    h = pl.program_id(0)
    lgf = jnp.full((1, 1), lgf_ref[h], F32)
    lgb = jnp.full((1, 1), lgb_ref[h], F32)

    row = lax.broadcasted_iota(jnp.int32, (c_len, c_len), 0)
    col = lax.broadcasted_iota(jnp.int32, (c_len, c_len), 1)
    diff = (row - col).astype(F32)
    intra = jnp.where(diff >= 0, jnp.exp(lgf * jnp.maximum(diff, 0.0)),
                      jnp.exp(lgb * jnp.maximum(-diff, 0.0)))
    pos = lax.broadcasted_iota(jnp.int32, (c_len, 1), 0).astype(F32)
    q_dec_f = jnp.exp(lgf * (pos + 1.0))
    k_dec_f = jnp.exp(lgf * (c_len - 1.0 - pos))
    chunk_dec_f = jnp.exp(lgf * c_len)
    q_dec_b = jnp.exp(lgb * (c_len - pos))
    k_dec_b = jnp.exp(lgb * pos)
    chunk_dec_b = jnp.exp(lgb * c_len)

    def chunk_slice(c):
        if isinstance(c, int):
            return pl.ds(c * c_len, c_len)
        return pl.ds(pl.multiple_of(c * c_len, c_len), c_len)

    def kv_term(c, k_dec):
        sl = chunk_slice(c)
        return _kv_outer((k_ref[sl, :].astype(F32) * k_dec).astype(BF16), v_ref[sl, :])

    def fwd_value(c, state):
        sl = chunk_slice(c)
        q = q_ref[sl, :]
        s = (_qk(q, k_ref[sl, :]) * intra).astype(BF16)
        qf = (q.astype(F32) * q_dec_f).astype(BF16)
        return (jnp.dot(s, v_ref[sl, :], preferred_element_type=F32)
                + jnp.dot(qf, state.astype(BF16), preferred_element_type=F32))

    def bwd_value(c, state):
        qb = (q_ref[chunk_slice(c), :].astype(F32) * q_dec_b).astype(BF16)
        return jnp.dot(qb, state.astype(BF16), preferred_element_type=F32)

    group = 2 if n_chunks % 4 == 0 else 1

    def make_scan(s_ref, kv_ref, chunk_dec, k_dec, chunk_of, value_fn):
        def init(s_init):
            s_ref[...] = jnp.zeros_like(s_ref)
            kv_ref[0] = s_init
            for g in range(1, group):
                kv_ref[g] = kv_term(chunk_of(g - 1), k_dec)

        def step(p0):
            state = s_ref[...]
            vals = []
            for g in range(group):
                state = state * chunk_dec + kv_ref[g]
                vals.append(value_fn(chunk_of(p0 + g), state))
            s_ref[...] = state
            for g in range(group):
                nxt = jnp.minimum(p0 + group - 1 + g, n_chunks - 1)
                kv_ref[g] = kv_term(chunk_of(nxt), k_dec)
            return vals

        return init, step

    fwd_init, fwd_step = make_scan(sf_ref, kvf_ref, chunk_dec_f, k_dec_f, lambda p: p, fwd_value)
    bwd_init, bwd_step = make_scan(sb_ref, kvb_ref, chunk_dec_b, k_dec_b,
                                   lambda p: n_chunks - 1 - p, bwd_value)

    if ctx_len:
        cpos = lax.broadcasted_iota(jnp.int32, (ctx_len, 1), 0).astype(F32)
        kc = kc_ref[...].astype(F32)
        vc = vc_ref[...]
        fwd_init(_kv_outer((kc * jnp.exp(lgf * (ctx_len - 1.0 - cpos))).astype(BF16), vc))
        bwd_init(_kv_outer((kc * jnp.exp(lgb * cpos)).astype(BF16), vc))
    else:
        zero = jnp.zeros(sf_ref.shape, F32)
        fwd_init(zero)
        bwd_init(zero)

    def finish(c, o):
        sl = chunk_slice(c)
        mu = jnp.mean(o, axis=-1, keepdims=True)
        cen = o - mu
        var = jnp.mean(cen * cen, axis=-1, keepdims=True)
        normed = cen * lax.rsqrt(var + EPS)
        o_ref[sl, :] = (normed * gnw_ref[...] * g_ref[sl, :].astype(F32)).astype(BF16)

    half = n_chunks // 2

    def first_half(i, carry):
        p0 = i * group
        for g, (fv, bv) in enumerate(zip(fwd_step(p0), bwd_step(p0))):
            acc_ref[chunk_slice(p0 + g), :] = fv
            acc_ref[chunk_slice(n_chunks - 1 - p0 - g), :] = bv
        return carry

    def second_half(i, carry):
        p0 = half + i * group
        for g, (fv, bv) in enumerate(zip(fwd_step(p0), bwd_step(p0))):
            lo, hi = n_chunks - 1 - p0 - g, p0 + g
            finish(hi, acc_ref[chunk_slice(hi), :] + fv)
            finish(lo, acc_ref[chunk_slice(lo), :] + bv)
        return carry

    lax.fori_loop(0, half // group, first_half, 0)
    lax.fori_loop(0, half // group, second_half, 0)


def _retention(qkv, gate, lg_f, lg_b, gn_w, batch, n_tok, hd, qkv_ctx=None, ctx_len=0):
    n_heads = lg_f.shape[0]
    d = n_heads * hd
    c_len = 2 * RET_CHUNK if n_tok % (8 * RET_CHUNK) == 0 else RET_CHUNK
    n_chunks = n_tok // c_len
    assert n_tok % (2 * c_len) == 0

    def spec(n_rows, part):
        return pl.BlockSpec((n_rows, hd), lambda h, b: (b, part * n_heads + h))

    smem = pl.BlockSpec(memory_space=pltpu.SMEM)
    in_specs = [smem, smem, spec(n_tok, 0), spec(n_tok, 1), spec(n_tok, 2), spec(n_tok, 0),
                pl.BlockSpec((1, hd), lambda h, b: (0, h))]
    args = [lg_f, lg_b, qkv, qkv, qkv, gate, gn_w.reshape(1, d)]
    if ctx_len:
        in_specs += [spec(ctx_len, 1), spec(ctx_len, 2)]
        args += [qkv_ctx, qkv_ctx]
    return pl.pallas_call(
        functools.partial(_retention_kernel, c_len=c_len, n_chunks=n_chunks, ctx_len=ctx_len),
        out_shape=jax.ShapeDtypeStruct((batch * n_tok, d), BF16),
        grid=(n_heads, batch),
        in_specs=in_specs,
        out_specs=pl.BlockSpec((n_tok, hd), lambda h, b: (b, h)),
        scratch_shapes=[pltpu.VMEM((n_tok, hd), F32),
                        pltpu.VMEM((hd, hd), F32),
                        pltpu.VMEM((hd, hd), F32),
                        pltpu.VMEM((2, hd, hd), F32),
                        pltpu.VMEM((2, hd, hd), F32)],
        compiler_params=_params(2),
        name="retention",
    )(*args)


def _rope_tables(seq, hd):
    t = jnp.arange(seq)
    row = (t // GRID_W).astype(F32)
    col = (t % GRID_W).astype(F32)
    axis_dim = hd // 2
    inv_freq = jnp.power(ROPE_BASE, -jnp.arange(0, axis_dim, 2, dtype=F32) / axis_dim)
    ang_r = row[:, None] * inv_freq[None, :]
    ang_c = col[:, None] * inv_freq[None, :]
    cos = jnp.concatenate([jnp.cos(ang_r), jnp.cos(ang_r), jnp.cos(ang_c), jnp.cos(ang_c)], axis=-1)
    sin = jnp.concatenate([-jnp.sin(ang_r), jnp.sin(ang_r), -jnp.sin(ang_c), jnp.sin(ang_c)], axis=-1)
    return cos, sin


def kernel(x, c, ctx, c_ctx, ada_w, ada_b, norm1_w, norm2_w, na_wqkv, na_wo, na_qnorm_w, na_knorm_w, na_rpb, ret_wqkvg, ret_wo, ret_decay_f, ret_decay_b, ret_gn_w, ffn_w13, ffn_w2, moe_router, moe_w13, moe_w2):
    batch, seq, d = x.shape
    ctx_len = ctx.shape[1]
    n_ctx = batch * ctx_len
    depth = ada_w.shape[0]
    na_hd = na_qnorm_w.shape[1]
    ret_heads = ret_decay_f.shape[1]
    ret_hd = d // ret_heads
    assert batch + 1 <= ADA_ROWS

    cond = jnp.zeros((ADA_ROWS, d), F32).at[:batch].set(c).at[batch].set(c_ctx)
    ada = _ada_params(cond, ada_w, ada_b)
    rope = _rope_tables(seq, ret_hd)
    moe_w2_flat = moe_w2.reshape(moe_w2.shape[0], -1, d)

    h = x.reshape(batch * seq, d)
    hc = ctx.reshape(n_ctx, d)
    for i in range(depth):
        j = i // 2
        ctx_out = i < depth - 1
        mods = ada[i].reshape(ADA_ROWS, 6, 1, d)
        sh1, sc1, g1, sh2, sc2, g2 = (mods[:batch, p] for p in range(6))
        csh1, csc1, cg1, csh2, csc2, cg2 = (mods[batch:batch + 1, p] for p in range(6))

        def bf16_layer(w):
            return w[j:j + 1].astype(BF16)

        u = _norm_mod(h, norm1_w[i], sh1, sc1, seq)
        uc = _norm_mod(hc, norm1_w[i], csh1, csc1, n_ctx)
        if i % 2 == 0:
            w_o = na_wo
            qkvc, w_qkv = _mm_na_qkv(uc, na_wqkv, j, na_qnorm_w[j], na_knorm_w[j], cast=True)
            qkv = _mm_na_qkv(u, w_qkv, 0, na_qnorm_w[j], na_knorm_w[j])
            o, oc = _na_attention(qkv, qkvc, na_rpb[j], batch, seq, ctx_len, na_hd)
        else:
            w_o = ret_wo
            lg_f = jax.nn.log_sigmoid(ret_decay_f[j].astype(F32))
            lg_b = jax.nn.log_sigmoid(ret_decay_b[j].astype(F32))
            qkvc, w_qkv = _mm_ret_proj(uc, ret_wqkvg, j, 0, 3 * d, ret_hd, None, n_ctx, False, cast=True)
            gc, w_g = _mm_ret_proj(uc, ret_wqkvg, j, 3 * d, d, ret_hd, None, n_ctx, True, cast=True)
            qkv = _mm_ret_proj(u, w_qkv, 0, 0, 3 * d, ret_hd, rope, seq, False)
            g = _mm_ret_proj(u, w_g, 0, 0, d, ret_hd, None, seq, True)
            o = _retention(qkv, g, lg_f, lg_b, ret_gn_w[j], batch, seq, ret_hd, qkvc, ctx_len)
            if ctx_out:
                oc = _retention(qkvc, gc, lg_f, lg_b, ret_gn_w[j], batch, ctx_len, ret_hd)
        if ctx_out:
            hc, w_o_b = _mm_res(oc, w_o, j, hc, cg1, n_ctx, cast=True)
        else:
            w_o_b = bf16_layer(w_o)
        h = _mm_res(o, w_o_b, 0, h, g1, seq)

        if i % 2 == 0:
            w13, w2, router = ffn_w13, ffn_w2, None
        else:
            w13, w2, router = moe_w13, moe_w2_flat, moe_router[j]
        comb = combc = None
        if ctx_out:
            uc2 = _norm_mod(hc, norm2_w[i], csh2, csc2, n_ctx, router)
            if router is not None:
                uc2, combc = uc2
            actc, w_a, w_b = _mm_swiglu(uc2, w13, w13, j, combc, cast=True)
            hc, w2_b = _mm_res(actc, w2, j, hc, cg2, n_ctx, cast=True)
        else:
            half = w13.shape[-1] // 2
            w_a, w_b = bf16_layer(w13[..., :half]), bf16_layer(w13[..., half:])
            w2_b = bf16_layer(w2)
        u2 = _norm_mod(h, norm2_w[i], sh2, sc2, seq, router)
        if router is not None:
            u2, comb = u2
        h = _mm_res(_mm_swiglu(u2, w_a, w_b, 0, comb), w2_b, 0, h, g2, seq)
    return h.reshape(batch, seq, d)
```

```python
import functools

import jax
import jax.numpy as jnp
from jax import lax
from jax.experimental import pallas as pl
from jax.experimental.pallas import tpu as pltpu

F32 = jnp.float32
BF16 = jnp.bfloat16

GRID_W = 64
RET_CHUNK = 128
ROPE_BASE = 10000.0
EPS = 1e-6
NEG_INF = -1e30
LOG2E = 1.4426950408889634

V7X_VMEM_BYTES = 64 * 1024 * 1024
VMEM_LIMIT_BYTES = V7X_VMEM_BYTES * 7 // 8
LANES = 128
ADA_ROWS = 8


def _params(n_grid_dims):
    return pltpu.CompilerParams(
        dimension_semantics=("arbitrary",) * n_grid_dims,
        vmem_limit_bytes=VMEM_LIMIT_BYTES,
    )


def _tile(n, pref):
    t = min(pref, n)
    while n % t:
        t //= 2
    return t


def _silu(x):
    return x * (1.0 / (1.0 + jnp.exp(-x)))


def _ada_kernel(x_ref, w_ref, b_ref, o_ref):
    xa = _silu(x_ref[...]).astype(BF16)
    w = w_ref[...].astype(BF16)
    o_ref[...] = jnp.dot(xa, w, preferred_element_type=F32) + b_ref[...]


def _ada_params(cond, ada_w, ada_b):
    depth, d, n = ada_w.shape
    bn = _tile(n, 512)
    return pl.pallas_call(
        _ada_kernel,
        out_shape=jax.ShapeDtypeStruct((depth, ADA_ROWS, n), F32),
        grid=(depth, n // bn),
        in_specs=[
            pl.BlockSpec((ADA_ROWS, d), lambda l, j: (0, 0)),
            pl.BlockSpec((None, d, bn), lambda l, j: (l, 0, j)),
            pl.BlockSpec((None, 1, bn), lambda l, j: (l, 0, j)),
        ],
        out_specs=pl.BlockSpec((None, ADA_ROWS, bn), lambda l, j: (l, 0, j)),
        compiler_params=_params(2),
        name="ada_params",
    )(cond, ada_w, ada_b.reshape(depth, 1, n))


def _norm_mod_kernel(x_ref, nw_ref, sh_ref, sc_ref, *rest, n_experts):
    x = x_ref[...]
    y = x * lax.rsqrt(jnp.mean(x * x, axis=-1, keepdims=True) + EPS)
    u = (y * nw_ref[...]) * (1.0 + sc_ref[...]) + sh_ref[...]
    ub = u.astype(BF16)
    if n_experts == 0:
        (u_ref,) = rest
        u_ref[...] = ub
        return
    wr_ref, u_ref, comb_ref = rest
    u_ref[...] = ub
    logits = jnp.dot(ub, wr_ref[...], preferred_element_type=F32)
    lane = lax.broadcasted_iota(jnp.int32, logits.shape, 1).astype(F32)
    valid = lane < n_experts
    logits = jnp.where(valid, logits, NEG_INF)
    e = jnp.exp(logits - jnp.max(logits, axis=-1, keepdims=True))
    probs = jnp.where(valid, e / jnp.sum(e, axis=-1, keepdims=True), -1.0)
    p1 = jnp.max(probs, axis=-1, keepdims=True)
    i1 = jnp.min(jnp.where(probs == p1, lane, float(LANES)), axis=-1, keepdims=True)
    rest_p = jnp.where(lane == i1, -1.0, probs)
    p2 = jnp.max(rest_p, axis=-1, keepdims=True)
    i2 = jnp.min(jnp.where(rest_p == p2, lane, float(LANES)), axis=-1, keepdims=True)
    inv = 1.0 / (p1 + p2)
    comb_ref[...] = jnp.where(lane == i1, p1 * inv, jnp.where(lane == i2, p2 * inv, 0.0))


def _norm_mod(x, norm_w, shift, scale, rows_per_batch, router_w=None):
    m, d = x.shape
    bm = _tile(rows_per_batch, 512)
    tiles_per_batch = rows_per_batch // bm
    mod_spec = pl.BlockSpec((None, 1, d), lambda i: (i // tiles_per_batch, 0, 0))
    in_specs = [
        pl.BlockSpec((bm, d), lambda i: (i, 0)),
        pl.BlockSpec((1, d), lambda i: (0, 0)),
        mod_spec,
        mod_spec,
    ]
    args = [x, norm_w.reshape(1, d), shift, scale]
    u_shape = jax.ShapeDtypeStruct((m, d), BF16)
    u_spec = pl.BlockSpec((bm, d), lambda i: (i, 0))
    if router_w is None:
        n_experts = 0
        out_shape, out_specs = u_shape, u_spec
    else:
        n_experts = router_w.shape[1]
        wr = jnp.zeros((d, LANES), BF16).at[:, :n_experts].set(router_w.astype(BF16))
        in_specs.append(pl.BlockSpec((d, LANES), lambda i: (0, 0)))
        args.append(wr)
        out_shape = (u_shape, jax.ShapeDtypeStruct((m, LANES), F32))
        out_specs = (u_spec, pl.BlockSpec((bm, LANES), lambda i: (i, 0)))
    return pl.pallas_call(
        functools.partial(_norm_mod_kernel, n_experts=n_experts),
        out_shape=out_shape,
        grid=(m // bm,),
        in_specs=in_specs,
        out_specs=out_specs,
        compiler_params=_params(1),
        name="norm_mod",
    )(*args)


def _weight_tile(w_ref, wout_ref):
    if wout_ref is None:
        return w_ref[...]
    w = w_ref[...].astype(BF16)
    wout_ref[...] = w
    return w


def _mm_na_qkv_kernel(x_ref, w_ref, qn_ref, kn_ref, o_ref, *wout, tiles_per_part, hd):
    j = pl.program_id(1)
    w = _weight_tile(w_ref, wout[0] if wout else None)
    acc = jnp.dot(x_ref[...], w, preferred_element_type=F32)
    n_heads = acc.shape[1] // hd

    is_q = j < tiles_per_part
    is_v = j >= 2 * tiles_per_part
    nw = jnp.where(is_q, qn_ref[...] * (hd ** -0.5 * LOG2E), kn_ref[...])
    for t in range(n_heads):
        a = acc[:, t * hd:(t + 1) * hd]
        y = a * lax.rsqrt(jnp.mean(a * a, axis=-1, keepdims=True) + EPS)
        o_ref[:, t * hd:(t + 1) * hd] = jnp.where(is_v, a, y * nw).astype(BF16)


def _cast_outputs(cast, out_shape, out_spec, w_shapes, w_specs):
    if not cast:
        return out_shape, out_spec
    return ((out_shape,) + tuple(jax.ShapeDtypeStruct(s, BF16) for s in w_shapes),
            (out_spec,) + tuple(w_specs))


def _mm_na_qkv(x, w, layer, qn_w, kn_w, cast=False):
    m, k = x.shape
    n = w.shape[2]
    hd = qn_w.shape[0]
    bm, bn = _tile(m, 1024), _tile(n // 3, 512 if cast else 1024)
    assert not cast or m == bm
    out_shape, out_specs = _cast_outputs(
        cast, jax.ShapeDtypeStruct((m, n), BF16), pl.BlockSpec((bm, bn), lambda i, j: (i, j)),
        [(1, k, n)], [pl.BlockSpec((None, k, bn), lambda i, j: (0, 0, j))])
    return pl.pallas_call(
        functools.partial(_mm_na_qkv_kernel, tiles_per_part=n // 3 // bn, hd=hd),
        out_shape=out_shape,
        grid=(m // bm, n // bn),
        in_specs=[
            pl.BlockSpec((bm, k), lambda i, j: (i, 0)),
            pl.BlockSpec((None, k, bn), lambda i, j: (layer, 0, j)),
            pl.BlockSpec((1, hd), lambda i, j: (0, 0)),
            pl.BlockSpec((1, hd), lambda i, j: (0, 0)),
        ],
        out_specs=out_specs,
        compiler_params=_params(2),
        name="mm_na_qkv",
    )(x, w, qn_w.reshape(1, hd), kn_w.reshape(1, hd))


def _mm_ret_proj_kernel(x_ref, w_ref, *rest, tiles_per_part, hd, rope, gate, cast):
    if rope:
        cos_ref, sin_ref = rest[:2]
        rest = rest[2:]
    o_ref = rest[0]
    j = pl.program_id(1)
    w = _weight_tile(w_ref, rest[1] if cast else None)
    acc = jnp.dot(x_ref[...], w, preferred_element_type=F32)
    if gate:
        o_ref[...] = _silu(acc).astype(BF16)
        return
    n_heads = acc.shape[1] // hd
    is_qk = j < 2 * tiles_per_part
    qk_scale = jnp.where(j >= tiles_per_part, hd ** -0.5, 1.0)
    for t in range(n_heads):
        a = acc[:, t * hd:(t + 1) * hd]
        qk = a
        if rope:
            lane = lax.broadcasted_iota(jnp.int32, a.shape, 1)
            first = (lane % (hd // 2)) < (hd // 4)
            partner = jnp.where(first, pltpu.roll(a, hd - hd // 4, 1), pltpu.roll(a, hd // 4, 1))
            qk = a * cos_ref[...] + partner * sin_ref[...]
        o_ref[:, t * hd:(t + 1) * hd] = jnp.where(is_qk, qk * qk_scale, a).astype(BF16)


def _mm_ret_proj(x, w, layer, col0, n, hd, rope_tables, seq, gate, cast=False):
    m, k = x.shape
    part = n if gate else n // 3
    bm, bn = _tile(seq, 1024), _tile(part, 512 if cast else 1024)
    assert not cast or m == bm
    assert col0 % bn == 0
    j0 = col0 // bn
    rope = rope_tables is not None and not gate
    in_specs = [
        pl.BlockSpec((bm, k), lambda i, j: (i, 0)),
        pl.BlockSpec((None, k, bn), lambda i, j: (layer, 0, j0 + j)),
    ]
    args = [x, w]
    if rope:
        tiles_per_seq = seq // bm
        tab_spec = pl.BlockSpec((bm, hd), lambda i, j: (i % tiles_per_seq, 0))
        in_specs += [tab_spec, tab_spec]
        args += list(rope_tables)
    out_shape, out_specs = _cast_outputs(
        cast, jax.ShapeDtypeStruct((m, n), BF16), pl.BlockSpec((bm, bn), lambda i, j: (i, j)),
        [(1, k, n)], [pl.BlockSpec((None, k, bn), lambda i, j: (0, 0, j))])
    return pl.pallas_call(
        functools.partial(_mm_ret_proj_kernel, tiles_per_part=part // bn, hd=hd, rope=rope, gate=gate,
                          cast=cast),
        out_shape=out_shape,
        grid=(m // bm, n // bn),
        in_specs=in_specs,
        out_specs=out_specs,
        compiler_params=_params(2),
        name="mm_ret_g" if gate else "mm_ret_qkv",
    )(*args)


def _mm_swiglu_kernel(x_ref, wa_ref, wb_ref, *rest, tiles_per_expert, cast):
    if tiles_per_expert:
        comb_ref = rest[0]
        rest = rest[1:]
    o_ref = rest[0]
    x = x_ref[...]
    a = jnp.dot(x, _weight_tile(wa_ref, rest[1] if cast else None), preferred_element_type=F32)
    b = jnp.dot(x, _weight_tile(wb_ref, rest[2] if cast else None), preferred_element_type=F32)
    act = _silu(a) * b
    if tiles_per_expert:
        e = pl.program_id(1) // tiles_per_expert
        comb = comb_ref[...]
        lane = lax.broadcasted_iota(jnp.int32, comb.shape, 1)
        act = act * jnp.sum(jnp.where(lane == e, comb, 0.0), axis=-1, keepdims=True)
    o_ref[...] = act.astype(BF16)


def _mm_swiglu(x, wa, wb, layer, comb=None, cast=False):
    m, k = x.shape
    bm = _tile(m, 1024)
    assert not cast or m == bm
    moe = comb is not None
    f = wa.shape[-1] // 2 if cast else wa.shape[-1]
    bn = _tile(f, 256 if cast else 512)
    nt = f // bn
    b_off = nt if cast else 0
    if moe:
        n_e = wa.shape[1]
        wa_spec = pl.BlockSpec((None, None, k, bn), lambda i, j: (layer, j // nt, 0, j % nt))
        wb_spec = pl.BlockSpec((None, None, k, bn), lambda i, j: (layer, j // nt, 0, b_off + j % nt))
        w_shape = (1, n_e, k, f)
        wo_spec = pl.BlockSpec((None, None, k, bn), lambda i, j: (0, j // nt, 0, j % nt))
        n_out = n_e * f
    else:
        wa_spec = pl.BlockSpec((None, k, bn), lambda i, j: (layer, 0, j))
        wb_spec = pl.BlockSpec((None, k, bn), lambda i, j: (layer, 0, b_off + j))
        w_shape = (1, k, f)
        wo_spec = pl.BlockSpec((None, k, bn), lambda i, j: (0, 0, j))
        n_out = f
    in_specs = [pl.BlockSpec((bm, k), lambda i, j: (i, 0)), wa_spec, wb_spec]
    args = [x, wa, wb]
    if moe:
        in_specs.append(pl.BlockSpec((bm, LANES), lambda i, j: (i, 0)))
        args.append(comb)
    out_shape, out_specs = _cast_outputs(
        cast, jax.ShapeDtypeStruct((m, n_out), BF16), pl.BlockSpec((bm, bn), lambda i, j: (i, j)),
        [w_shape, w_shape], [wo_spec, wo_spec])
    return pl.pallas_call(
        functools.partial(_mm_swiglu_kernel, tiles_per_expert=nt if moe else 0, cast=cast),
        out_shape=out_shape,
        grid=(m // bm, n_out // bn),
        in_specs=in_specs,
        out_specs=out_specs,
        compiler_params=_params(2),
        name="mm_swiglu",
    )(*args)


def _mm_res_kernel(x_ref, w_ref, res_ref, gate_ref, o_ref, *wout):
    w = _weight_tile(w_ref, wout[0] if wout else None)
    acc = jnp.dot(x_ref[...], w, preferred_element_type=F32)
    o_ref[...] = res_ref[...] + gate_ref[...] * acc


def _mm_res(x, w, layer, res, gate, rows_per_batch, cast=False):
    m, k = x.shape
    n = w.shape[2]
    bm, bn = _tile(rows_per_batch, 1024), _tile(n, 512 if cast else 1024)
    assert not cast or m == bm
    tiles_per_batch = rows_per_batch // bm
    out_shape, out_specs = _cast_outputs(
        cast, jax.ShapeDtypeStruct((m, n), F32), pl.BlockSpec((bm, bn), lambda i, j: (i, j)),
        [(1, k, n)], [pl.BlockSpec((None, k, bn), lambda i, j: (0, 0, j))])
    return pl.pallas_call(
        _mm_res_kernel,
        out_shape=out_shape,
        grid=(m // bm, n // bn),
        in_specs=[
            pl.BlockSpec((bm, k), lambda i, j: (i, 0)),
            pl.BlockSpec((None, k, bn), lambda i, j: (layer, 0, j)),
            pl.BlockSpec((bm, bn), lambda i, j: (i, j)),
            pl.BlockSpec((None, 1, bn), lambda i, j: (i // tiles_per_batch, 0, j)),
        ],
        out_specs=out_specs,
        compiler_params=_params(2),
        name="mm_res",
    )(x, w, res, gate)


def _softmax_pv(scores, values):
    m = functools.reduce(jnp.maximum, [jnp.max(s, axis=-1, keepdims=True) for s in scores])
    out = functools.reduce(
        jnp.add,
        [jnp.dot(jnp.exp2(s - m).astype(BF16), v, preferred_element_type=F32)
         for s, v in zip(scores, values)])
    hd = out.shape[1] // 2
    return out[:, :hd] / out[:, hd:]


def _qk(q, k):
    return lax.dot_general(q, k, (((1,), (1,)), ((), ())), preferred_element_type=F32)


NA_Q_ROWS = 4
NA_KEY_ROWS = 12


def _na_block_geometry(t, rows, win_r):
    u0 = min(max(t * NA_Q_ROWS - win_r // 2, 0), rows - NA_KEY_ROWS)
    geo = []
    for a in range(NA_Q_ROWS):
        r = t * NA_Q_ROWS + a
        r0 = min(max(r - win_r // 2, 0), rows - win_r)
        geo.append([(r0 <= u0 + j < r0 + win_r, u0 + j - r + win_r - 1) for j in range(NA_KEY_ROWS)])
    return u0, geo


def _na_build_slabs(pairs_ref, slab_ref, rows, win_r):
    n_blk = rows // NA_Q_ROWS
    lane = lax.broadcasted_iota(jnp.int32, (GRID_W, 2 * GRID_W), 1)
    neg = jnp.full((GRID_W, 2 * GRID_W), NEG_INF, F32)
    for v, t in enumerate((0, 1, n_blk - 1)):
        _, geo = _na_block_geometry(t, rows, win_r)
        for a in range(NA_Q_ROWS):
            for jj in range(NA_KEY_ROWS // 2):
                (ok0, d0), (ok1, _) = geo[a][2 * jj], geo[a][2 * jj + 1]
                if ok0 and ok1:
                    tile = pairs_ref[d0 + 1]
                elif ok0:
                    tile = jnp.where(lane < GRID_W, pairs_ref[d0 + 1], neg)
                elif ok1:
                    tile = jnp.where(lane >= GRID_W, pairs_ref[d0 + 1], neg)
                else:
                    tile = neg
                slab_ref[v, a * GRID_W:(a + 1) * GRID_W, jj * 2 * GRID_W:(jj + 1) * 2 * GRID_W] = tile


def _na_attn_kernel(q_ref, k_ref, v_ref, qc_ref, kc_ref, vc_ref, pairs_ref, o_ref, oc_ref,
                    bias_ref, vaug_ref, vcaug_ref, *, rows, win_r):
    hd = v_ref.shape[1]

    @pl.when(pl.program_id(1) == 0)
    def _():
        _na_build_slabs(pairs_ref, bias_ref, rows, win_r)

    @pl.when((pl.program_id(0) == 0) & (pl.program_id(1) == 0))
    def _():
        vaug_ref[:, hd:] = jnp.ones((vaug_ref.shape[0], hd), BF16)
        vcaug_ref[:, hd:] = jnp.ones((vcaug_ref.shape[0], hd), BF16)

    vaug_ref[:, :hd] = v_ref[...]
    vcaug_ref[:, :hd] = vc_ref[...]
    kc = kc_ref[...]
    vc = vcaug_ref[...]
    n_blk = rows // NA_Q_ROWS
    n_q = NA_Q_ROWS * GRID_W
    n_k = NA_KEY_ROWS * GRID_W

    def query_block(t, carry):
        u0 = jnp.clip(t * NA_Q_ROWS - win_r // 2, 0, rows - NA_KEY_ROWS)
        slab = jnp.where(t == 0, 0, jnp.where(t == n_blk - 1, 2, 1))
        q_rows = pl.ds(pl.multiple_of(t * n_q, n_q), n_q)
        k_rows = pl.ds(pl.multiple_of(u0 * GRID_W, GRID_W), n_k)
        q = q_ref[q_rows, :]
        s_win = _qk(q, k_ref[k_rows, :]) + bias_ref[slab]
        o = _softmax_pv([s_win, _qk(q, kc)], [vaug_ref[k_rows, :], vc])
        o_ref[q_rows, :] = o.astype(BF16)
        return carry

    lax.fori_loop(0, n_blk, query_block, 0, unroll=16 if n_blk % 16 == 0 else 4)
    oc_ref[...] = _softmax_pv([_qk(qc_ref[...], kc)], [vc]).astype(BF16)


def _na_bias_pairs(rpb):
    n_heads, n_dr, n_dc = rpb.shape
    win_c = (n_dc + 1) // 2
    cols = jnp.arange(GRID_W)
    c0 = jnp.clip(cols - win_c // 2, 0, GRID_W - win_c)
    col_in = (cols[None, :] >= c0[:, None]) & (cols[None, :] < c0[:, None] + win_c)
    dc = jnp.clip(cols[None, :] - cols[:, None] + win_c - 1, 0, 2 * win_c - 2)
    masked = jnp.where(col_in, rpb[:, :, dc].astype(F32) * LOG2E, NEG_INF)
    neg = jnp.full((n_heads, 1, GRID_W, GRID_W), NEG_INF, F32)
    ext = jnp.concatenate([neg, masked, neg], axis=1)
    return jnp.concatenate([ext[:, :-1], ext[:, 1:]], axis=-1)


def _na_attention(qkv, qkvc, rpb, batch, seq, ctx_len, hd):
    n_heads = rpb.shape[0]
    win_r = (rpb.shape[1] + 1) // 2
    rows = seq // GRID_W
    n_blk = rows // NA_Q_ROWS
    assert rows % NA_Q_ROWS == 0 and n_blk >= 3 and NA_Q_ROWS >= win_r // 2
    assert (n_blk - 2) * NA_Q_ROWS - win_r // 2 <= rows - NA_KEY_ROWS
    assert NA_KEY_ROWS >= win_r + NA_Q_ROWS - 1 and NA_KEY_ROWS % 2 == 0
    d = n_heads * hd
    pairs = _na_bias_pairs(rpb)

    def spec(n_rows, part):
        return pl.BlockSpec((n_rows, hd), lambda h, b: (b, part * n_heads + h))

    return pl.pallas_call(
        functools.partial(_na_attn_kernel, rows=rows, win_r=win_r),
        out_shape=(jax.ShapeDtypeStruct((batch * seq, d), BF16),
                   jax.ShapeDtypeStruct((batch * ctx_len, d), BF16)),
        grid=(n_heads, batch),
        in_specs=[spec(seq, 0), spec(seq, 1), spec(seq, 2),
                  spec(ctx_len, 0), spec(ctx_len, 1), spec(ctx_len, 2),
                  pl.BlockSpec((None,) + pairs.shape[1:], lambda h, b: (h, 0, 0, 0))],
        out_specs=(pl.BlockSpec((seq, hd), lambda h, b: (b, h)),
                   pl.BlockSpec((ctx_len, hd), lambda h, b: (b, h))),
        scratch_shapes=[pltpu.VMEM((3, NA_Q_ROWS * GRID_W, NA_KEY_ROWS * GRID_W), F32),
                        pltpu.VMEM((seq, 2 * hd), BF16),
                        pltpu.VMEM((ctx_len, 2 * hd), BF16)],
        compiler_params=_params(2),
        name="na_attention",
    )(qkv, qkv, qkv, qkvc, qkvc, qkvc, pairs)


def _kv_outer(k, v):
    return lax.dot_general(k, v, (((0,), (0,)), ((), ())), preferred_element_type=F32)


def _retention_kernel(lgf_ref, lgb_ref, q_ref, k_ref, v_ref, g_ref, gnw_ref, *rest,
                      c_len, n_chunks, ctx_len):
    if ctx_len:
        kc_ref, vc_ref, o_ref, acc_ref, sf_ref, sb_ref, kvf_ref, kvb_ref = rest
    else:
        o_ref, acc_ref, sf_ref, sb_ref, kvf_ref, kvb_ref = rest
    h = pl.program_id(0)
    lgf = jnp.full((1, 1), lgf_ref[h], F32)
    lgb = jnp.full((1, 1), lgb_ref[h], F32)

    row = lax.broadcasted_iota(jnp.int32, (c_len, c_len), 0)
    col = lax.broadcasted_iota(jnp.int32, (c_len, c_len), 1)
    diff = (row - col).astype(F32)
    intra = jnp.where(diff >= 0, jnp.exp(lgf * jnp.maximum(diff, 0.0)),
                      jnp.exp(lgb * jnp.maximum(-diff, 0.0)))
    pos = lax.broadcasted_iota(jnp.int32, (c_len, 1), 0).astype(F32)
    q_dec_f = jnp.exp(lgf * (pos + 1.0))
    k_dec_f = jnp.exp(lgf * (c_len - 1.0 - pos))
    chunk_dec_f = jnp.exp(lgf * c_len)
    q_dec_b = jnp.exp(lgb * (c_len - pos))
    k_dec_b = jnp.exp(lgb * pos)
    chunk_dec_b = jnp.exp(lgb * c_len)

    def chunk_slice(c):
        if isinstance(c, int):
            return pl.ds(c * c_len, c_len)
        return pl.ds(pl.multiple_of(c * c_len, c_len), c_len)

    def kv_term(c, k_dec):
        sl = chunk_slice(c)
        return _kv_outer((k_ref[sl, :].astype(F32) * k_dec).astype(BF16), v_ref[sl, :])

    def fwd_value(c, state):
        sl = chunk_slice(c)
        q = q_ref[sl, :]
        s = (_qk(q, k_ref[sl, :]) * intra).astype(BF16)
        qf = (q.astype(F32) * q_dec_f).astype(BF16)
        return (jnp.dot(s, v_ref[sl, :], preferred_element_type=F32)
                + jnp.dot(qf, state.astype(BF16), preferred_element_type=F32))

    def bwd_value(c, state):
        qb = (q_ref[chunk_slice(c), :].astype(F32) * q_dec_b).astype(BF16)
        return jnp.dot(qb, state.astype(BF16), preferred_element_type=F32)

    group = 2 if n_chunks % 4 == 0 else 1

    def make_scan(s_ref, kv_ref, chunk_dec, k_dec, chunk_of, value_fn):
        def init(s_init):
            s_ref[...] = jnp.zeros_like(s_ref)
            kv_ref[0] = s_init
            for g in range(1, group):
                kv_ref[g] = kv_term(chunk_of(g - 1), k_dec)

        def step(p0):
            state = s_ref[...]
            vals = []
            for g in range(group):
                state = state * chunk_dec + kv_ref[g]
                vals.append(value_fn(chunk_of(p0 + g), state))
            s_ref[...] = state
            for g in range(group):
                nxt = jnp.minimum(p0 + group - 1 + g, n_chunks - 1)
                kv_ref[g] = kv_term(chunk_of(nxt), k_dec)
            return vals

        return init, step

    fwd_init, fwd_step = make_scan(sf_ref, kvf_ref, chunk_dec_f, k_dec_f, lambda p: p, fwd_value)
    bwd_init, bwd_step = make_scan(sb_ref, kvb_ref, chunk_dec_b, k_dec_b,
                                   lambda p: n_chunks - 1 - p, bwd_value)

    if ctx_len:
        cpos = lax.broadcasted_iota(jnp.int32, (ctx_len, 1), 0).astype(F32)
        kc = kc_ref[...].astype(F32)
        vc = vc_ref[...]
        fwd_init(_kv_outer((kc * jnp.exp(lgf * (ctx_len - 1.0 - cpos))).astype(BF16), vc))
        bwd_init(_kv_outer((kc * jnp.exp(lgb * cpos)).astype(BF16), vc))
    else:
        zero = jnp.zeros(sf_ref.shape, F32)
        fwd_init(zero)
        bwd_init(zero)

    def finish(c, o):
        sl = chunk_slice(c)
        mu = jnp.mean(o, axis=-1, keepdims=True)
        cen = o - mu
        var = jnp.mean(cen * cen, axis=-1, keepdims=True)
        normed = cen * lax.rsqrt(var + EPS)
        o_ref[sl, :] = (normed * gnw_ref[...] * g_ref[sl, :].astype(F32)).astype(BF16)

    half = n_chunks // 2

    def first_half(i, carry):
        p0 = i * group
        for g, (fv, bv) in enumerate(zip(fwd_step(p0), bwd_step(p0))):
            acc_ref[chunk_slice(p0 + g), :] = fv
            acc_ref[chunk_slice(n_chunks - 1 - p0 - g), :] = bv
        return carry

    def second_half(i, carry):
        p0 = half + i * group
        for g, (fv, bv) in enumerate(zip(fwd_step(p0), bwd_step(p0))):
            lo, hi = n_chunks - 1 - p0 - g, p0 + g
            finish(hi, acc_ref[chunk_slice(hi), :] + fv)
            finish(lo, acc_ref[chunk_slice(lo), :] + bv)
        return carry

    lax.fori_loop(0, half // group, first_half, 0, unroll=True)
    lax.fori_loop(0, half // group, second_half, 0, unroll=True)


def _retention(qkv, gate, lg_f, lg_b, gn_w, batch, n_tok, hd, qkv_ctx=None, ctx_len=0):
    n_heads = lg_f.shape[0]
    d = n_heads * hd
    c_len = 2 * RET_CHUNK if n_tok % (8 * RET_CHUNK) == 0 else RET_CHUNK
    n_chunks = n_tok // c_len
    assert n_tok % (2 * c_len) == 0

    def spec(n_rows, part):
        return pl.BlockSpec((n_rows, hd), lambda h, b: (b, part * n_heads + h))

    smem = pl.BlockSpec(memory_space=pltpu.SMEM)
    in_specs = [smem, smem, spec(n_tok, 0), spec(n_tok, 1), spec(n_tok, 2), spec(n_tok, 0),
                pl.BlockSpec((1, hd), lambda h, b: (0, h))]
    args = [lg_f, lg_b, qkv, qkv, qkv, gate, gn_w.reshape(1, d)]
    if ctx_len:
        in_specs += [spec(ctx_len, 1), spec(ctx_len, 2)]
        args += [qkv_ctx, qkv_ctx]
    return pl.pallas_call(
        functools.partial(_retention_kernel, c_len=c_len, n_chunks=n_chunks, ctx_len=ctx_len),
        out_shape=jax.ShapeDtypeStruct((batch * n_tok, d), BF16),
        grid=(n_heads, batch),
        in_specs=in_specs,
        out_specs=pl.BlockSpec((n_tok, hd), lambda h, b: (b, h)),
        scratch_shapes=[pltpu.VMEM((n_tok, hd), F32),
                        pltpu.VMEM((hd, hd), F32),
                        pltpu.VMEM((hd, hd), F32),
                        pltpu.VMEM((2, hd, hd), F32),
                        pltpu.VMEM((2, hd, hd), F32)],
        compiler_params=_params(2),
        name="retention",
    )(*args)


def _rope_tables(seq, hd):
    t = jnp.arange(seq)
    row = (t // GRID_W).astype(F32)
    col = (t % GRID_W).astype(F32)
    axis_dim = hd // 2
    inv_freq = jnp.power(ROPE_BASE, -jnp.arange(0, axis_dim, 2, dtype=F32) / axis_dim)
    ang_r = row[:, None] * inv_freq[None, :]
    ang_c = col[:, None] * inv_freq[None, :]
    cos = jnp.concatenate([jnp.cos(ang_r), jnp.cos(ang_r), jnp.cos(ang_c), jnp.cos(ang_c)], axis=-1)
    sin = jnp.concatenate([-jnp.sin(ang_r), jnp.sin(ang_r), -jnp.sin(ang_c), jnp.sin(ang_c)], axis=-1)
    return cos, sin


def kernel(x, c, ctx, c_ctx, ada_w, ada_b, norm1_w, norm2_w, na_wqkv, na_wo, na_qnorm_w, na_knorm_w, na_rpb, ret_wqkvg, ret_wo, ret_decay_f, ret_decay_b, ret_gn_w, ffn_w13, ffn_w2, moe_router, moe_w13, moe_w2):
    batch, seq, d = x.shape
    ctx_len = ctx.shape[1]
    n_ctx = batch * ctx_len
    depth = ada_w.shape[0]
    na_hd = na_qnorm_w.shape[1]
    ret_heads = ret_decay_f.shape[1]
    ret_hd = d // ret_heads
    assert batch + 1 <= ADA_ROWS

    cond = jnp.zeros((ADA_ROWS, d), F32).at[:batch].set(c).at[batch].set(c_ctx)
    ada = _ada_params(cond, ada_w, ada_b)
    rope = _rope_tables(seq, ret_hd)
    moe_w2_flat = moe_w2.reshape(moe_w2.shape[0], -1, d)

    h = x.reshape(batch * seq, d)
    hc = ctx.reshape(n_ctx, d)
    for i in range(depth):
        j = i // 2
        ctx_out = i < depth - 1
        mods = ada[i].reshape(ADA_ROWS, 6, 1, d)
        sh1, sc1, g1, sh2, sc2, g2 = (mods[:batch, p] for p in range(6))
        csh1, csc1, cg1, csh2, csc2, cg2 = (mods[batch:batch + 1, p] for p in range(6))

        def bf16_layer(w):
            return w[j:j + 1].astype(BF16)

        u = _norm_mod(h, norm1_w[i], sh1, sc1, seq)
        uc = _norm_mod(hc, norm1_w[i], csh1, csc1, n_ctx)
        if i % 2 == 0:
            w_o = na_wo
            qkvc, w_qkv = _mm_na_qkv(uc, na_wqkv, j, na_qnorm_w[j], na_knorm_w[j], cast=True)
            qkv = _mm_na_qkv(u, w_qkv, 0, na_qnorm_w[j], na_knorm_w[j])
            o, oc = _na_attention(qkv, qkvc, na_rpb[j], batch, seq, ctx_len, na_hd)
        else:
            w_o = ret_wo
            lg_f = jax.nn.log_sigmoid(ret_decay_f[j].astype(F32))
            lg_b = jax.nn.log_sigmoid(ret_decay_b[j].astype(F32))
            qkvc, w_qkv = _mm_ret_proj(uc, ret_wqkvg, j, 0, 3 * d, ret_hd, None, n_ctx, False, cast=True)
            gc, w_g = _mm_ret_proj(uc, ret_wqkvg, j, 3 * d, d, ret_hd, None, n_ctx, True, cast=True)
            qkv = _mm_ret_proj(u, w_qkv, 0, 0, 3 * d, ret_hd, rope, seq, False)
            g = _mm_ret_proj(u, w_g, 0, 0, d, ret_hd, None, seq, True)
            o = _retention(qkv, g, lg_f, lg_b, ret_gn_w[j], batch, seq, ret_hd, qkvc, ctx_len)
            if ctx_out:
                oc = _retention(qkvc, gc, lg_f, lg_b, ret_gn_w[j], batch, ctx_len, ret_hd)
        if ctx_out:
            hc, w_o_b = _mm_res(oc, w_o, j, hc, cg1, n_ctx, cast=True)
        else:
            w_o_b = bf16_layer(w_o)
        h = _mm_res(o, w_o_b, 0, h, g1, seq)

        if i % 2 == 0:
            w13, w2, router = ffn_w13, ffn_w2, None
        else:
            w13, w2, router = moe_w13, moe_w2_flat, moe_router[j]
        comb = combc = None
        if ctx_out:
            uc2 = _norm_mod(hc, norm2_w[i], csh2, csc2, n_ctx, router)
            if router is not None:
                uc2, combc = uc2
            actc, w_a, w_b = _mm_swiglu(uc2, w13, w13, j, combc, cast=True)
            hc, w2_b = _mm_res(actc, w2, j, hc, cg2, n_ctx, cast=True)
        else:
            half = w13.shape[-1] // 2
            w_a, w_b = bf16_layer(w13[..., :half]), bf16_layer(w13[..., half:])
            w2_b = bf16_layer(w2)
        u2 = _norm_mod(h, norm2_w[i], sh2, sc2, seq, router)
        if router is not None:
            u2, comb = u2
        h = _mm_res(_mm_swiglu(u2, w_a, w_b, 0, comb), w2_b, 0, h, g2, seq)
    return h.reshape(batch, seq, d)
```

```python
import functools

import jax
import jax.numpy as jnp
from jax import lax
from jax.experimental import pallas as pl
from jax.experimental.pallas import tpu as pltpu

F32 = jnp.float32
BF16 = jnp.bfloat16

GRID_W = 64
RET_CHUNK = 128
ROPE_BASE = 10000.0
EPS = 1e-6
NEG_INF = -1e30
LOG2E = 1.4426950408889634

V7X_VMEM_BYTES = 64 * 1024 * 1024
VMEM_LIMIT_BYTES = V7X_VMEM_BYTES * 7 // 8
LANES = 128
ADA_ROWS = 8


def _params(n_grid_dims):
    return pltpu.CompilerParams(
        dimension_semantics=("arbitrary",) * n_grid_dims,
        vmem_limit_bytes=VMEM_LIMIT_BYTES,
    )


def _tile(n, pref):
    t = min(pref, n)
    while n % t:
        t //= 2
    return t


def _silu(x):
    return x * (1.0 / (1.0 + jnp.exp(-x)))


def _ada_kernel(x_ref, w_ref, b_ref, o_ref):
    xa = _silu(x_ref[...]).astype(BF16)
    w = w_ref[...].astype(BF16)
    o_ref[...] = jnp.dot(xa, w, preferred_element_type=F32) + b_ref[...]


def _ada_params(cond, ada_w, ada_b):
    depth, d, n = ada_w.shape
    bn = _tile(n, 512)
    return pl.pallas_call(
        _ada_kernel,
        out_shape=jax.ShapeDtypeStruct((depth, ADA_ROWS, n), F32),
        grid=(depth, n // bn),
        in_specs=[
            pl.BlockSpec((ADA_ROWS, d), lambda l, j: (0, 0)),
            pl.BlockSpec((None, d, bn), lambda l, j: (l, 0, j)),
            pl.BlockSpec((None, 1, bn), lambda l, j: (l, 0, j)),
        ],
        out_specs=pl.BlockSpec((None, ADA_ROWS, bn), lambda l, j: (l, 0, j)),
        compiler_params=_params(2),
        name="ada_params",
    )(cond, ada_w, ada_b.reshape(depth, 1, n))


def _norm_mod_kernel(x_ref, nw_ref, sh_ref, sc_ref, *rest, n_experts):
    x = x_ref[...]
    y = x * lax.rsqrt(jnp.mean(x * x, axis=-1, keepdims=True) + EPS)
    u = (y * nw_ref[...]) * (1.0 + sc_ref[...]) + sh_ref[...]
    ub = u.astype(BF16)
    if n_experts == 0:
        (u_ref,) = rest
        u_ref[...] = ub
        return
    wr_ref, u_ref, comb_ref = rest
    u_ref[...] = ub
    logits = jnp.dot(ub, wr_ref[...], preferred_element_type=F32)
    lane = lax.broadcasted_iota(jnp.int32, logits.shape, 1).astype(F32)
    valid = lane < n_experts
    logits = jnp.where(valid, logits, NEG_INF)
    e = jnp.exp(logits - jnp.max(logits, axis=-1, keepdims=True))
    probs = jnp.where(valid, e / jnp.sum(e, axis=-1, keepdims=True), -1.0)
    p1 = jnp.max(probs, axis=-1, keepdims=True)
    i1 = jnp.min(jnp.where(probs == p1, lane, float(LANES)), axis=-1, keepdims=True)
    rest_p = jnp.where(lane == i1, -1.0, probs)
    p2 = jnp.max(rest_p, axis=-1, keepdims=True)
    i2 = jnp.min(jnp.where(rest_p == p2, lane, float(LANES)), axis=-1, keepdims=True)
    inv = 1.0 / (p1 + p2)
    comb_ref[...] = jnp.where(lane == i1, p1 * inv, jnp.where(lane == i2, p2 * inv, 0.0))


def _norm_mod(x, norm_w, shift, scale, rows_per_batch, router_w=None):
    m, d = x.shape
    bm = _tile(rows_per_batch, min(512, max(16, m // 8)))
    tiles_per_batch = rows_per_batch // bm
    mod_spec = pl.BlockSpec((None, 1, d), lambda i: (i // tiles_per_batch, 0, 0))
    in_specs = [
        pl.BlockSpec((bm, d), lambda i: (i, 0)),
        pl.BlockSpec((1, d), lambda i: (0, 0)),
        mod_spec,
        mod_spec,
    ]
    args = [x, norm_w.reshape(1, d), shift, scale]
    u_shape = jax.ShapeDtypeStruct((m, d), BF16)
    u_spec = pl.BlockSpec((bm, d), lambda i: (i, 0))
    if router_w is None:
        n_experts = 0
        out_shape, out_specs = u_shape, u_spec
    else:
        n_experts = router_w.shape[1]
        wr = jnp.zeros((d, LANES), BF16).at[:, :n_experts].set(router_w.astype(BF16))
        in_specs.append(pl.BlockSpec((d, LANES), lambda i: (0, 0)))
        args.append(wr)
        out_shape = (u_shape, jax.ShapeDtypeStruct((m, LANES), F32))
        out_specs = (u_spec, pl.BlockSpec((bm, LANES), lambda i: (i, 0)))
    return pl.pallas_call(
        functools.partial(_norm_mod_kernel, n_experts=n_experts),
        out_shape=out_shape,
        grid=(m // bm,),
        in_specs=in_specs,
        out_specs=out_specs,
        compiler_params=_params(1),
        name="norm_mod",
    )(*args)


def _weight_tile(w_ref, wout_ref):
    if wout_ref is None:
        return w_ref[...]
    w = w_ref[...].astype(BF16)
    wout_ref[...] = w
    return w


def _mm_na_qkv_kernel(x_ref, w_ref, qn_ref, kn_ref, o_ref, *wout, tiles_per_part, hd):
    j = pl.program_id(1)
    w = _weight_tile(w_ref, wout[0] if wout else None)
    acc = jnp.dot(x_ref[...], w, preferred_element_type=F32)
    n_heads = acc.shape[1] // hd

    is_q = j < tiles_per_part
    is_v = j >= 2 * tiles_per_part
    nw = jnp.where(is_q, qn_ref[...] * (hd ** -0.5 * LOG2E), kn_ref[...])
    for t in range(n_heads):
        a = acc[:, t * hd:(t + 1) * hd]
        y = a * lax.rsqrt(jnp.mean(a * a, axis=-1, keepdims=True) + EPS)
        o_ref[:, t * hd:(t + 1) * hd] = jnp.where(is_v, a, y * nw).astype(BF16)


def _cast_outputs(cast, out_shape, out_spec, w_shapes, w_specs):
    if not cast:
        return out_shape, out_spec
    return ((out_shape,) + tuple(jax.ShapeDtypeStruct(s, BF16) for s in w_shapes),
            (out_spec,) + tuple(w_specs))


def _mm_na_qkv(x, w, layer, qn_w, kn_w, cast=False):
    m, k = x.shape
    n = w.shape[2]
    hd = qn_w.shape[0]
    bm, bn = _tile(m, 1024), _tile(n // 3, 512 if cast else 1024)
    assert not cast or m == bm
    out_shape, out_specs = _cast_outputs(
        cast, jax.ShapeDtypeStruct((m, n), BF16), pl.BlockSpec((bm, bn), lambda i, j: (i, j)),
        [(1, k, n)], [pl.BlockSpec((None, k, bn), lambda i, j: (0, 0, j))])
    return pl.pallas_call(
        functools.partial(_mm_na_qkv_kernel, tiles_per_part=n // 3 // bn, hd=hd),
        out_shape=out_shape,
        grid=(m // bm, n // bn),
        in_specs=[
            pl.BlockSpec((bm, k), lambda i, j: (i, 0)),
            pl.BlockSpec((None, k, bn), lambda i, j: (layer, 0, j)),
            pl.BlockSpec((1, hd), lambda i, j: (0, 0)),
            pl.BlockSpec((1, hd), lambda i, j: (0, 0)),
        ],
        out_specs=out_specs,
        compiler_params=_params(2),
        name="mm_na_qkv",
    )(x, w, qn_w.reshape(1, hd), kn_w.reshape(1, hd))


def _mm_ret_proj_kernel(x_ref, w_ref, *rest, tiles_per_part, hd, rope, gate, cast):
    if rope:
        cos_ref, sin_ref = rest[:2]
        rest = rest[2:]
    o_ref = rest[0]
    j = pl.program_id(1)
    w = _weight_tile(w_ref, rest[1] if cast else None)
    acc = jnp.dot(x_ref[...], w, preferred_element_type=F32)
    if gate:
        o_ref[...] = _silu(acc).astype(BF16)
        return
    n_heads = acc.shape[1] // hd
    is_qk = j < 2 * tiles_per_part
    qk_scale = jnp.where(j >= tiles_per_part, hd ** -0.5, 1.0)
    for t in range(n_heads):
        a = acc[:, t * hd:(t + 1) * hd]
        qk = a
        if rope:
            lane = lax.broadcasted_iota(jnp.int32, a.shape, 1)
            first = (lane % (hd // 2)) < (hd // 4)
            partner = jnp.where(first, pltpu.roll(a, hd - hd // 4, 1), pltpu.roll(a, hd // 4, 1))
            qk = a * cos_ref[...] + partner * sin_ref[...]
        o_ref[:, t * hd:(t + 1) * hd] = jnp.where(is_qk, qk * qk_scale, a).astype(BF16)


def _mm_ret_proj(x, w, layer, col0, n, hd, rope_tables, seq, gate, cast=False):
    m, k = x.shape
    part = n if gate else n // 3
    bm, bn = _tile(seq, 1024), _tile(part, 512 if cast else 1024)
    assert not cast or m == bm
    assert col0 % bn == 0
    j0 = col0 // bn
    rope = rope_tables is not None and not gate
    in_specs = [
        pl.BlockSpec((bm, k), lambda i, j: (i, 0)),
        pl.BlockSpec((None, k, bn), lambda i, j: (layer, 0, j0 + j)),
    ]
    args = [x, w]
    if rope:
        tiles_per_seq = seq // bm
        tab_spec = pl.BlockSpec((bm, hd), lambda i, j: (i % tiles_per_seq, 0))
        in_specs += [tab_spec, tab_spec]
        args += list(rope_tables)
    out_shape, out_specs = _cast_outputs(
        cast, jax.ShapeDtypeStruct((m, n), BF16), pl.BlockSpec((bm, bn), lambda i, j: (i, j)),
        [(1, k, n)], [pl.BlockSpec((None, k, bn), lambda i, j: (0, 0, j))])
    return pl.pallas_call(
        functools.partial(_mm_ret_proj_kernel, tiles_per_part=part // bn, hd=hd, rope=rope, gate=gate,
                          cast=cast),
        out_shape=out_shape,
        grid=(m // bm, n // bn),
        in_specs=in_specs,
        out_specs=out_specs,
        compiler_params=_params(2),
        name="mm_ret_g" if gate else "mm_ret_qkv",
    )(*args)


def _mm_swiglu_kernel(x_ref, wa_ref, wb_ref, *rest, tiles_per_expert, cast):
    if tiles_per_expert:
        comb_ref = rest[0]
        rest = rest[1:]
    o_ref = rest[0]
    x = x_ref[...]
    a = jnp.dot(x, _weight_tile(wa_ref, rest[1] if cast else None), preferred_element_type=F32)
    b = jnp.dot(x, _weight_tile(wb_ref, rest[2] if cast else None), preferred_element_type=F32)
    act = _silu(a) * b
    if tiles_per_expert:
        e = pl.program_id(1) // tiles_per_expert
        comb = comb_ref[...]
        lane = lax.broadcasted_iota(jnp.int32, comb.shape, 1)
        act = act * jnp.sum(jnp.where(lane == e, comb, 0.0), axis=-1, keepdims=True)
    o_ref[...] = act.astype(BF16)


def _mm_swiglu(x, wa, wb, layer, comb=None, cast=False):
    m, k = x.shape
    bm = _tile(m, 1024)
    assert not cast or (m == bm and wa is wb)
    moe = comb is not None
    f = wa.shape[-1] // 2 if wa is wb else wa.shape[-1]
    bn = _tile(f, 256 if cast else 512)
    nt = f // bn
    b_off = nt if wa is wb else 0
    if moe:
        n_e = wa.shape[1]
        wa_spec = pl.BlockSpec((None, None, k, bn), lambda i, j: (layer, j // nt, 0, j % nt))
        wb_spec = pl.BlockSpec((None, None, k, bn), lambda i, j: (layer, j // nt, 0, b_off + j % nt))
        w_shape = (1, n_e, k, f)
        wo_spec = pl.BlockSpec((None, None, k, bn), lambda i, j: (0, j // nt, 0, j % nt))
        n_out = n_e * f
    else:
        wa_spec = pl.BlockSpec((None, k, bn), lambda i, j: (layer, 0, j))
        wb_spec = pl.BlockSpec((None, k, bn), lambda i, j: (layer, 0, b_off + j))
        w_shape = (1, k, f)
        wo_spec = pl.BlockSpec((None, k, bn), lambda i, j: (0, 0, j))
        n_out = f
    in_specs = [pl.BlockSpec((bm, k), lambda i, j: (i, 0)), wa_spec, wb_spec]
    args = [x, wa, wb]
    if moe:
        in_specs.append(pl.BlockSpec((bm, LANES), lambda i, j: (i, 0)))
        args.append(comb)
    out_shape, out_specs = _cast_outputs(
        cast, jax.ShapeDtypeStruct((m, n_out), BF16), pl.BlockSpec((bm, bn), lambda i, j: (i, j)),
        [w_shape, w_shape], [wo_spec, wo_spec])
    return pl.pallas_call(
        functools.partial(_mm_swiglu_kernel, tiles_per_expert=nt if moe else 0, cast=cast),
        out_shape=out_shape,
        grid=(m // bm, n_out // bn),
        in_specs=in_specs,
        out_specs=out_specs,
        compiler_params=_params(2),
        name="mm_swiglu",
    )(*args)


def _mm_res_kernel(x_ref, w_ref, res_ref, gate_ref, o_ref, *wout):
    w = _weight_tile(w_ref, wout[0] if wout else None)
    acc = jnp.dot(x_ref[...], w, preferred_element_type=F32)
    o_ref[...] = res_ref[...] + gate_ref[...] * acc


def _mm_res(x, w, layer, res, gate, rows_per_batch, cast=False):
    m, k = x.shape
    n = w.shape[2]
    bm, bn = _tile(rows_per_batch, 1024), _tile(n, 512 if cast else 1024)
    assert not cast or m == bm
    tiles_per_batch = rows_per_batch // bm
    out_shape, out_specs = _cast_outputs(
        cast, jax.ShapeDtypeStruct((m, n), F32), pl.BlockSpec((bm, bn), lambda i, j: (i, j)),
        [(1, k, n)], [pl.BlockSpec((None, k, bn), lambda i, j: (0, 0, j))])
    return pl.pallas_call(
        _mm_res_kernel,
        out_shape=out_shape,
        grid=(m // bm, n // bn),
        in_specs=[
            pl.BlockSpec((bm, k), lambda i, j: (i, 0)),
            pl.BlockSpec((None, k, bn), lambda i, j: (layer, 0, j)),
            pl.BlockSpec((bm, bn), lambda i, j: (i, j)),
            pl.BlockSpec((None, 1, bn), lambda i, j: (i // tiles_per_batch, 0, j)),
        ],
        out_specs=out_specs,
        compiler_params=_params(2),
        name="mm_res",
    )(x, w, res, gate)


def _softmax_pv(scores, values):
    m = functools.reduce(jnp.maximum, [jnp.max(s, axis=-1, keepdims=True) for s in scores])
    out = functools.reduce(
        jnp.add,
        [jnp.dot(jnp.exp2(s - m).astype(BF16), v, preferred_element_type=F32)
         for s, v in zip(scores, values)])
    hd = out.shape[1] // 2
    return out[:, :hd] / out[:, hd:]


def _qk(q, k):
    return lax.dot_general(q, k, (((1,), (1,)), ((), ())), preferred_element_type=F32)


NA_Q_ROWS = 4
NA_KEY_ROWS = 12


def _na_block_geometry(t, rows, win_r):
    u0 = min(max(t * NA_Q_ROWS - win_r // 2, 0), rows - NA_KEY_ROWS)
    geo = []
    for a in range(NA_Q_ROWS):
        r = t * NA_Q_ROWS + a
        r0 = min(max(r - win_r // 2, 0), rows - win_r)
        geo.append([(r0 <= u0 + j < r0 + win_r, u0 + j - r + win_r - 1) for j in range(NA_KEY_ROWS)])
    return u0, geo


def _na_build_slabs(pairs_ref, slab_ref, rows, win_r):
    n_blk = rows // NA_Q_ROWS
    lane = lax.broadcasted_iota(jnp.int32, (GRID_W, 2 * GRID_W), 1)
    neg = jnp.full((GRID_W, 2 * GRID_W), NEG_INF, F32)
    for v, t in enumerate((0, 1, n_blk - 1)):
        _, geo = _na_block_geometry(t, rows, win_r)
        for a in range(NA_Q_ROWS):
            for jj in range(NA_KEY_ROWS // 2):
                (ok0, d0), (ok1, _) = geo[a][2 * jj], geo[a][2 * jj + 1]
                if ok0 and ok1:
                    tile = pairs_ref[d0 + 1]
                elif ok0:
                    tile = jnp.where(lane < GRID_W, pairs_ref[d0 + 1], neg)
                elif ok1:
                    tile = jnp.where(lane >= GRID_W, pairs_ref[d0 + 1], neg)
                else:
                    tile = neg
                slab_ref[v, a * GRID_W:(a + 1) * GRID_W, jj * 2 * GRID_W:(jj + 1) * 2 * GRID_W] = tile


def _na_attn_kernel(q_ref, k_ref, v_ref, qc_ref, kc_ref, vc_ref, pairs_ref, o_ref, oc_ref,
                    bias_ref, vaug_ref, vcaug_ref, *, rows, win_r):
    hd = v_ref.shape[1]

    @pl.when(pl.program_id(1) == 0)
    def _():
        _na_build_slabs(pairs_ref, bias_ref, rows, win_r)

    @pl.when((pl.program_id(0) == 0) & (pl.program_id(1) == 0))
    def _():
        vaug_ref[:, hd:] = jnp.ones((vaug_ref.shape[0], hd), BF16)
        vcaug_ref[:, hd:] = jnp.ones((vcaug_ref.shape[0], hd), BF16)

    vaug_ref[:, :hd] = v_ref[...]
    vcaug_ref[:, :hd] = vc_ref[...]
    kc = kc_ref[...]
    vc = vcaug_ref[...]
    n_blk = rows // NA_Q_ROWS
    n_q = NA_Q_ROWS * GRID_W
    n_k = NA_KEY_ROWS * GRID_W

    def query_block(t, carry):
        u0 = jnp.clip(t * NA_Q_ROWS - win_r // 2, 0, rows - NA_KEY_ROWS)
        slab = jnp.where(t == 0, 0, jnp.where(t == n_blk - 1, 2, 1))
        q_rows = pl.ds(pl.multiple_of(t * n_q, n_q), n_q)
        k_rows = pl.ds(pl.multiple_of(u0 * GRID_W, GRID_W), n_k)
        q = q_ref[q_rows, :]
        s_win = _qk(q, k_ref[k_rows, :]) + bias_ref[slab]
        o = _softmax_pv([s_win, _qk(q, kc)], [vaug_ref[k_rows, :], vc])
        o_ref[q_rows, :] = o.astype(BF16)
        return carry

    lax.fori_loop(0, n_blk, query_block, 0, unroll=16 if n_blk % 16 == 0 else 4)
    oc_ref[...] = _softmax_pv([_qk(qc_ref[...], kc)], [vc]).astype(BF16)


def _na_bias_pairs(rpb):
    n_heads, n_dr, n_dc = rpb.shape
    win_c = (n_dc + 1) // 2
    cols = jnp.arange(GRID_W)
    c0 = jnp.clip(cols - win_c // 2, 0, GRID_W - win_c)
    col_in = (cols[None, :] >= c0[:, None]) & (cols[None, :] < c0[:, None] + win_c)
    dc = jnp.clip(cols[None, :] - cols[:, None] + win_c - 1, 0, 2 * win_c - 2)
    masked = jnp.where(col_in, rpb[:, :, dc].astype(F32) * LOG2E, NEG_INF)
    neg = jnp.full((n_heads, 1, GRID_W, GRID_W), NEG_INF, F32)
    ext = jnp.concatenate([neg, masked, neg], axis=1)
    return jnp.concatenate([ext[:, :-1], ext[:, 1:]], axis=-1)


def _na_attention(qkv, qkvc, rpb, batch, seq, ctx_len, hd):
    n_heads = rpb.shape[0]
    win_r = (rpb.shape[1] + 1) // 2
    rows = seq // GRID_W
    n_blk = rows // NA_Q_ROWS
    assert rows % NA_Q_ROWS == 0 and n_blk >= 3 and NA_Q_ROWS >= win_r // 2
    assert (n_blk - 2) * NA_Q_ROWS - win_r // 2 <= rows - NA_KEY_ROWS
    assert NA_KEY_ROWS >= win_r + NA_Q_ROWS - 1 and NA_KEY_ROWS % 2 == 0
    d = n_heads * hd
    pairs = _na_bias_pairs(rpb)

    def spec(n_rows, part):
        return pl.BlockSpec((n_rows, hd), lambda h, b: (b, part * n_heads + h))

    return pl.pallas_call(
        functools.partial(_na_attn_kernel, rows=rows, win_r=win_r),
        out_shape=(jax.ShapeDtypeStruct((batch * seq, d), BF16),
                   jax.ShapeDtypeStruct((batch * ctx_len, d), BF16)),
        grid=(n_heads, batch),
        in_specs=[spec(seq, 0), spec(seq, 1), spec(seq, 2),
                  spec(ctx_len, 0), spec(ctx_len, 1), spec(ctx_len, 2),
                  pl.BlockSpec((None,) + pairs.shape[1:], lambda h, b: (h, 0, 0, 0))],
        out_specs=(pl.BlockSpec((seq, hd), lambda h, b: (b, h)),
                   pl.BlockSpec((ctx_len, hd), lambda h, b: (b, h))),
        scratch_shapes=[pltpu.VMEM((3, NA_Q_ROWS * GRID_W, NA_KEY_ROWS * GRID_W), F32),
                        pltpu.VMEM((seq, 2 * hd), BF16),
                        pltpu.VMEM((ctx_len, 2 * hd), BF16)],
        compiler_params=_params(2),
        name="na_attention",
    )(qkv, qkv, qkv, qkvc, qkvc, qkvc, pairs)


def _kv_outer(k, v):
    return lax.dot_general(k, v, (((0,), (0,)), ((), ())), preferred_element_type=F32)


def _retention_kernel(lgf_ref, lgb_ref, q_ref, k_ref, v_ref, g_ref, gnw_ref, *rest,
                      c_len, n_chunks, ctx_len):
    if ctx_len:
        kc_ref, vc_ref, o_ref, acc_ref, sf_ref, sb_ref, kvf_ref, kvb_ref = rest
    else:
        o_ref, acc_ref, sf_ref, sb_ref, kvf_ref, kvb_ref = rest
    h = pl.program_id(0)
    lgf = jnp.full((1, 1), lgf_ref[h], F32)
    lgb = jnp.full((1, 1), lgb_ref[h], F32)

    row = lax.broadcasted_iota(jnp.int32, (c_len, c_len), 0)
    col = lax.broadcasted_iota(jnp.int32, (c_len, c_len), 1)
    diff = (row - col).astype(F32)
    intra = jnp.where(diff >= 0, jnp.exp(lgf * jnp.maximum(diff, 0.0)),
                      jnp.exp(lgb * jnp.maximum(-diff, 0.0)))
    pos = lax.broadcasted_iota(jnp.int32, (c_len, 1), 0).astype(F32)
    q_dec_f = jnp.exp(lgf * (pos + 1.0))
    k_dec_f = jnp.exp(lgf * (c_len - 1.0 - pos))
    chunk_dec_f = jnp.exp(lgf * c_len)
    q_dec_b = jnp.exp(lgb * (c_len - pos))
    k_dec_b = jnp.exp(lgb * pos)
    chunk_dec_b = jnp.exp(lgb * c_len)

    def chunk_slice(c):
        if isinstance(c, int):
            return pl.ds(c * c_len, c_len)
        return pl.ds(pl.multiple_of(c * c_len, c_len), c_len)

    def kv_term(c, k_dec):
        sl = chunk_slice(c)
        return _kv_outer((k_ref[sl, :].astype(F32) * k_dec).astype(BF16), v_ref[sl, :])

    def fwd_value(c, state):
        sl = chunk_slice(c)
        q = q_ref[sl, :]
        s = (_qk(q, k_ref[sl, :]) * intra).astype(BF16)
        qf = (q.astype(F32) * q_dec_f).astype(BF16)
        return (jnp.dot(s, v_ref[sl, :], preferred_element_type=F32)
                + jnp.dot(qf, state.astype(BF16), preferred_element_type=F32))

    def bwd_value(c, state):
        qb = (q_ref[chunk_slice(c), :].astype(F32) * q_dec_b).astype(BF16)
        return jnp.dot(qb, state.astype(BF16), preferred_element_type=F32)

    group = 2 if n_chunks % 4 == 0 else 1

    def make_scan(s_ref, kv_ref, chunk_dec, k_dec, chunk_of, value_fn):
        def init(s_init):
            s_ref[...] = jnp.zeros_like(s_ref)
            kv_ref[0] = s_init
            for g in range(1, group):
                kv_ref[g] = kv_term(chunk_of(g - 1), k_dec)

        def step(p0):
            state = s_ref[...]
            vals = []
            for g in range(group):
                state = state * chunk_dec + kv_ref[g]
                vals.append(value_fn(chunk_of(p0 + g), state))
            s_ref[...] = state
            for g in range(group):
                nxt = jnp.minimum(p0 + group - 1 + g, n_chunks - 1)
                kv_ref[g] = kv_term(chunk_of(nxt), k_dec)
            return vals

        return init, step

    fwd_init, fwd_step = make_scan(sf_ref, kvf_ref, chunk_dec_f, k_dec_f, lambda p: p, fwd_value)
    bwd_init, bwd_step = make_scan(sb_ref, kvb_ref, chunk_dec_b, k_dec_b,
                                   lambda p: n_chunks - 1 - p, bwd_value)

    if ctx_len:
        cpos = lax.broadcasted_iota(jnp.int32, (ctx_len, 1), 0).astype(F32)
        kc = kc_ref[...].astype(F32)
        vc = vc_ref[...]
        fwd_init(_kv_outer((kc * jnp.exp(lgf * (ctx_len - 1.0 - cpos))).astype(BF16), vc))
        bwd_init(_kv_outer((kc * jnp.exp(lgb * cpos)).astype(BF16), vc))
    else:
        zero = jnp.zeros(sf_ref.shape, F32)
        fwd_init(zero)
        bwd_init(zero)

    def finish(c, o):
        sl = chunk_slice(c)
        mu = jnp.mean(o, axis=-1, keepdims=True)
        cen = o - mu
        var = jnp.mean(cen * cen, axis=-1, keepdims=True)
        normed = cen * lax.rsqrt(var + EPS)
        o_ref[sl, :] = (normed * gnw_ref[...] * g_ref[sl, :].astype(F32)).astype(BF16)

    half = n_chunks // 2

    def first_half(i, carry):
        p0 = i * group
        for g, (fv, bv) in enumerate(zip(fwd_step(p0), bwd_step(p0))):
            acc_ref[chunk_slice(p0 + g), :] = fv
            acc_ref[chunk_slice(n_chunks - 1 - p0 - g), :] = bv
        return carry

    def second_half(i, carry):
        p0 = half + i * group
        for g, (fv, bv) in enumerate(zip(fwd_step(p0), bwd_step(p0))):
            lo, hi = n_chunks - 1 - p0 - g, p0 + g
            finish(hi, acc_ref[chunk_slice(hi), :] + fv)
            finish(lo, acc_ref[chunk_slice(lo), :] + bv)
        return carry

    lax.fori_loop(0, half // group, first_half, 0, unroll=True)
    lax.fori_loop(0, half // group, second_half, 0, unroll=True)


def _retention(qkv, gate, lg_f, lg_b, gn_w, batch, n_tok, hd, qkv_ctx=None, ctx_len=0):
    n_heads = lg_f.shape[0]
    d = n_heads * hd
    c_len = 2 * RET_CHUNK if n_tok % (8 * RET_CHUNK) == 0 else RET_CHUNK
    n_chunks = n_tok // c_len
    assert n_tok % (2 * c_len) == 0

    def spec(n_rows, part):
        return pl.BlockSpec((n_rows, hd), lambda h, b: (b, part * n_heads + h))

    smem = pl.BlockSpec(memory_space=pltpu.SMEM)
    in_specs = [smem, smem, spec(n_tok, 0), spec(n_tok, 1), spec(n_tok, 2), spec(n_tok, 0),
                pl.BlockSpec((1, hd), lambda h, b: (0, h))]
    args = [lg_f, lg_b, qkv, qkv, qkv, gate, gn_w.reshape(1, d)]
    if ctx_len:
        in_specs += [spec(ctx_len, 1), spec(ctx_len, 2)]
        args += [qkv_ctx, qkv_ctx]
    return pl.pallas_call(
        functools.partial(_retention_kernel, c_len=c_len, n_chunks=n_chunks, ctx_len=ctx_len),
        out_shape=jax.ShapeDtypeStruct((batch * n_tok, d), BF16),
        grid=(n_heads, batch),
        in_specs=in_specs,
        out_specs=pl.BlockSpec((n_tok, hd), lambda h, b: (b, h)),
        scratch_shapes=[pltpu.VMEM((n_tok, hd), F32),
                        pltpu.VMEM((hd, hd), F32),
                        pltpu.VMEM((hd, hd), F32),
                        pltpu.VMEM((2, hd, hd), F32),
                        pltpu.VMEM((2, hd, hd), F32)],
        compiler_params=_params(2),
        name="retention",
    )(*args)


def _rope_tables(seq, hd):
    t = jnp.arange(seq)
    row = (t // GRID_W).astype(F32)
    col = (t % GRID_W).astype(F32)
    axis_dim = hd // 2
    inv_freq = jnp.power(ROPE_BASE, -jnp.arange(0, axis_dim, 2, dtype=F32) / axis_dim)
    ang_r = row[:, None] * inv_freq[None, :]
    ang_c = col[:, None] * inv_freq[None, :]
    cos = jnp.concatenate([jnp.cos(ang_r), jnp.cos(ang_r), jnp.cos(ang_c), jnp.cos(ang_c)], axis=-1)
    sin = jnp.concatenate([-jnp.sin(ang_r), jnp.sin(ang_r), -jnp.sin(ang_c), jnp.sin(ang_c)], axis=-1)
    return cos, sin


def kernel(x, c, ctx, c_ctx, ada_w, ada_b, norm1_w, norm2_w, na_wqkv, na_wo, na_qnorm_w, na_knorm_w, na_rpb, ret_wqkvg, ret_wo, ret_decay_f, ret_decay_b, ret_gn_w, ffn_w13, ffn_w2, moe_router, moe_w13, moe_w2):
    batch, seq, d = x.shape
    ctx_len = ctx.shape[1]
    n_ctx = batch * ctx_len
    depth = ada_w.shape[0]
    na_hd = na_qnorm_w.shape[1]
    ret_heads = ret_decay_f.shape[1]
    ret_hd = d // ret_heads
    assert batch + 1 <= ADA_ROWS

    cond = jnp.zeros((ADA_ROWS, d), F32).at[:batch].set(c).at[batch].set(c_ctx)
    ada = _ada_params(cond, ada_w, ada_b)
    rope = _rope_tables(seq, ret_hd)
    moe_w2_flat = moe_w2.reshape(moe_w2.shape[0], -1, d)

    h = x.reshape(batch * seq, d)
    hc = ctx.reshape(n_ctx, d)
    for i in range(depth):
        j = i // 2
        ctx_out = i < depth - 1
        mods = ada[i].reshape(ADA_ROWS, 6, 1, d)
        sh1, sc1, g1, sh2, sc2, g2 = (mods[:batch, p] for p in range(6))
        csh1, csc1, cg1, csh2, csc2, cg2 = (mods[batch:batch + 1, p] for p in range(6))

        def bf16_layer(w):
            return w[j:j + 1].astype(BF16)

        u = _norm_mod(h, norm1_w[i], sh1, sc1, seq)
        uc = _norm_mod(hc, norm1_w[i], csh1, csc1, n_ctx)
        if i % 2 == 0:
            w_o = na_wo
            qkvc, w_qkv = _mm_na_qkv(uc, na_wqkv, j, na_qnorm_w[j], na_knorm_w[j], cast=True)
            qkv = _mm_na_qkv(u, w_qkv, 0, na_qnorm_w[j], na_knorm_w[j])
            o, oc = _na_attention(qkv, qkvc, na_rpb[j], batch, seq, ctx_len, na_hd)
        else:
            w_o = ret_wo
            lg_f = jax.nn.log_sigmoid(ret_decay_f[j].astype(F32))
            lg_b = jax.nn.log_sigmoid(ret_decay_b[j].astype(F32))
            qkvc, w_qkv = _mm_ret_proj(uc, ret_wqkvg, j, 0, 3 * d, ret_hd, None, n_ctx, False, cast=True)
            gc, w_g = _mm_ret_proj(uc, ret_wqkvg, j, 3 * d, d, ret_hd, None, n_ctx, True, cast=True)
            qkv = _mm_ret_proj(u, w_qkv, 0, 0, 3 * d, ret_hd, rope, seq, False)
            g = _mm_ret_proj(u, w_g, 0, 0, d, ret_hd, None, seq, True)
            o = _retention(qkv, g, lg_f, lg_b, ret_gn_w[j], batch, seq, ret_hd, qkvc, ctx_len)
            if ctx_out:
                oc = _retention(qkvc, gc, lg_f, lg_b, ret_gn_w[j], batch, ctx_len, ret_hd)
        if ctx_out:
            hc, w_o_b = _mm_res(oc, w_o, j, hc, cg1, n_ctx, cast=True)
        else:
            w_o_b = bf16_layer(w_o)
        h = _mm_res(o, w_o_b, 0, h, g1, seq)

        if i % 2 == 0:
            w13, w2, router = ffn_w13, ffn_w2, None
        else:
            w13, w2, router = moe_w13, moe_w2_flat, moe_router[j]
        comb = combc = None
        if ctx_out:
            uc2 = _norm_mod(hc, norm2_w[i], csh2, csc2, n_ctx, router)
            if router is not None:
                uc2, combc = uc2
            actc, w_a, w_b = _mm_swiglu(uc2, w13, w13, j, combc, cast=True)
            hc, w2_b = _mm_res(actc, w2, j, hc, cg2, n_ctx, cast=True)
        else:
            w_a = w_b = bf16_layer(w13)
            w2_b = bf16_layer(w2)
        u2 = _norm_mod(h, norm2_w[i], sh2, sc2, seq, router)
        if router is not None:
            u2, comb = u2
        h = _mm_res(_mm_swiglu(u2, w_a, w_b, 0, comb), w2_b, 0, h, g2, seq)
    return h.reshape(batch, seq, d)
```

```python
import functools

import jax
import jax.numpy as jnp
from jax import lax
from jax.experimental import pallas as pl
from jax.experimental.pallas import tpu as pltpu

F32 = jnp.float32
BF16 = jnp.bfloat16

GRID_W = 64
RET_CHUNK = 128
ROPE_BASE = 10000.0
EPS = 1e-6
NEG_INF = -1e30
LOG2E = 1.4426950408889634

V7X_VMEM_BYTES = 64 * 1024 * 1024
VMEM_LIMIT_BYTES = V7X_VMEM_BYTES * 7 // 8
LANES = 128
ADA_ROWS = 8


def _params(n_grid_dims):
    return pltpu.CompilerParams(
        dimension_semantics=("arbitrary",) * n_grid_dims,
        vmem_limit_bytes=VMEM_LIMIT_BYTES,
    )


def _tile(n, pref):
    t = min(pref, n)
    while n % t:
        t //= 2
    return t


def _silu(x):
    return x * (1.0 / (1.0 + jnp.exp(-x)))


def _ada_kernel(x_ref, w_ref, b_ref, o_ref):
    xa = _silu(x_ref[...]).astype(BF16)
    w = w_ref[...].astype(BF16)
    o_ref[...] = jnp.dot(xa, w, preferred_element_type=F32) + b_ref[...]


def _ada_params(cond, ada_w, ada_b):
    depth, d, n = ada_w.shape
    bn = _tile(n, 512)
    return pl.pallas_call(
        _ada_kernel,
        out_shape=jax.ShapeDtypeStruct((depth, ADA_ROWS, n), F32),
        grid=(depth, n // bn),
        in_specs=[
            pl.BlockSpec((ADA_ROWS, d), lambda l, j: (0, 0)),
            pl.BlockSpec((None, d, bn), lambda l, j: (l, 0, j)),
            pl.BlockSpec((None, 1, bn), lambda l, j: (l, 0, j)),
        ],
        out_specs=pl.BlockSpec((None, ADA_ROWS, bn), lambda l, j: (l, 0, j)),
        compiler_params=_params(2),
        name="ada_params",
    )(cond, ada_w, ada_b.reshape(depth, 1, n))


SUBLANES = 8
NORM_ROWS = 16


def _norm_mod_kernel(x_ref, nw_ref, sh_ref, sc_ref, *rest, n_experts):
    if n_experts == 0:
        u_ref, gain_ref, shift_ref = rest
    else:
        wr_ref, u_ref, comb_ref, gain_ref, shift_ref = rest
    d = x_ref.shape[1]
    gain_ref[...] = jnp.broadcast_to(nw_ref[...] * (1.0 + sc_ref[...]), (SUBLANES, d))
    shift_ref[...] = jnp.broadcast_to(sh_ref[...], (SUBLANES, d))

    def row_group(r, carry):
        halves = []
        for s in range(NORM_ROWS // SUBLANES):
            rows = pl.ds(pl.multiple_of(r * NORM_ROWS + s * SUBLANES, SUBLANES), SUBLANES)
            x = x_ref[rows, :]
            inv_rms = lax.rsqrt(jnp.mean(x * x, axis=-1, keepdims=True) + EPS)
            halves.append((x_ref[rows, :] * inv_rms) * gain_ref[...] + shift_ref[...])
        rows = pl.ds(pl.multiple_of(r * NORM_ROWS, NORM_ROWS), NORM_ROWS)
        u_ref[rows, :] = jnp.concatenate(halves, axis=0).astype(BF16)
        return carry

    lax.fori_loop(0, x_ref.shape[0] // NORM_ROWS, row_group, 0, unroll=4)
    if n_experts == 0:
        return
    logits = jnp.dot(u_ref[...], wr_ref[...], preferred_element_type=F32)
    lane = lax.broadcasted_iota(jnp.int32, logits.shape, 1).astype(F32)
    valid = lane < n_experts
    logits = jnp.where(valid, logits, NEG_INF)
    e = jnp.exp(logits - jnp.max(logits, axis=-1, keepdims=True))
    probs = jnp.where(valid, e / jnp.sum(e, axis=-1, keepdims=True), -1.0)
    p1 = jnp.max(probs, axis=-1, keepdims=True)
    i1 = jnp.min(jnp.where(probs == p1, lane, float(LANES)), axis=-1, keepdims=True)
    rest_p = jnp.where(lane == i1, -1.0, probs)
    p2 = jnp.max(rest_p, axis=-1, keepdims=True)
    i2 = jnp.min(jnp.where(rest_p == p2, lane, float(LANES)), axis=-1, keepdims=True)
    inv = 1.0 / (p1 + p2)
    comb_ref[...] = jnp.where(lane == i1, p1 * inv, jnp.where(lane == i2, p2 * inv, 0.0))


def _norm_mod(x, norm_w, shift, scale, rows_per_batch, router_w=None):
    m, d = x.shape
    bm = _tile(rows_per_batch, min(512, max(16, m // 8)))
    tiles_per_batch = rows_per_batch // bm
    mod_spec = pl.BlockSpec((None, 1, d), lambda i: (i // tiles_per_batch, 0, 0))
    in_specs = [
        pl.BlockSpec((bm, d), lambda i: (i, 0)),
        pl.BlockSpec((1, d), lambda i: (0, 0)),
        mod_spec,
        mod_spec,
    ]
    args = [x, norm_w.reshape(1, d), shift, scale]
    u_shape = jax.ShapeDtypeStruct((m, d), BF16)
    u_spec = pl.BlockSpec((bm, d), lambda i: (i, 0))
    if router_w is None:
        n_experts = 0
        out_shape, out_specs = u_shape, u_spec
    else:
        n_experts = router_w.shape[1]
        wr = jnp.zeros((d, LANES), BF16).at[:, :n_experts].set(router_w.astype(BF16))
        in_specs.append(pl.BlockSpec((d, LANES), lambda i: (0, 0)))
        args.append(wr)
        out_shape = (u_shape, jax.ShapeDtypeStruct((m, LANES), F32))
        out_specs = (u_spec, pl.BlockSpec((bm, LANES), lambda i: (i, 0)))
    return pl.pallas_call(
        functools.partial(_norm_mod_kernel, n_experts=n_experts),
        out_shape=out_shape,
        grid=(m // bm,),
        in_specs=in_specs,
        out_specs=out_specs,
        scratch_shapes=[pltpu.VMEM((SUBLANES, d), F32), pltpu.VMEM((SUBLANES, d), F32)],
        compiler_params=_params(1),
        name="norm_mod",
    )(*args)


def _weight_tile(w_ref, wout_ref):
    if wout_ref is None:
        return w_ref[...]
    w = w_ref[...].astype(BF16)
    wout_ref[...] = w
    return w


def _mm_na_qkv_kernel(x_ref, w_ref, qn_ref, kn_ref, o_ref, *wout, tiles_per_part, hd):
    j = pl.program_id(1)
    w = _weight_tile(w_ref, wout[0] if wout else None)
    acc = jnp.dot(x_ref[...], w, preferred_element_type=F32)
    n_heads = acc.shape[1] // hd

    is_q = j < tiles_per_part
    is_v = j >= 2 * tiles_per_part
    nw = jnp.where(is_q, qn_ref[...] * (hd ** -0.5 * LOG2E), kn_ref[...])
    for t in range(n_heads):
        a = acc[:, t * hd:(t + 1) * hd]
        y = a * lax.rsqrt(jnp.mean(a * a, axis=-1, keepdims=True) + EPS)
        o_ref[:, t * hd:(t + 1) * hd] = jnp.where(is_v, a, y * nw).astype(BF16)


def _cast_outputs(cast, out_shape, out_spec, w_shapes, w_specs):
    if not cast:
        return out_shape, out_spec
    return ((out_shape,) + tuple(jax.ShapeDtypeStruct(s, BF16) for s in w_shapes),
            (out_spec,) + tuple(w_specs))


def _mm_na_qkv(x, w, layer, qn_w, kn_w, cast=False):
    m, k = x.shape
    n = w.shape[2]
    hd = qn_w.shape[0]
    bm, bn = _tile(m, 1024), _tile(n // 3, 512 if cast else 1024)
    assert not cast or m == bm
    out_shape, out_specs = _cast_outputs(
        cast, jax.ShapeDtypeStruct((m, n), BF16), pl.BlockSpec((bm, bn), lambda i, j: (i, j)),
        [(1, k, n)], [pl.BlockSpec((None, k, bn), lambda i, j: (0, 0, j))])
    return pl.pallas_call(
        functools.partial(_mm_na_qkv_kernel, tiles_per_part=n // 3 // bn, hd=hd),
        out_shape=out_shape,
        grid=(m // bm, n // bn),
        in_specs=[
            pl.BlockSpec((bm, k), lambda i, j: (i, 0)),
            pl.BlockSpec((None, k, bn), lambda i, j: (layer, 0, j)),
            pl.BlockSpec((1, hd), lambda i, j: (0, 0)),
            pl.BlockSpec((1, hd), lambda i, j: (0, 0)),
        ],
        out_specs=out_specs,
        compiler_params=_params(2),
        name="mm_na_qkv",
    )(x, w, qn_w.reshape(1, hd), kn_w.reshape(1, hd))


def _mm_ret_proj_kernel(x_ref, w_ref, *rest, tiles_per_part, hd, rope, gate, cast):
    if rope:
        cos_ref, sin_ref = rest[:2]
        rest = rest[2:]
    o_ref = rest[0]
    j = pl.program_id(1)
    w = _weight_tile(w_ref, rest[1] if cast else None)
    acc = jnp.dot(x_ref[...], w, preferred_element_type=F32)
    if gate:
        o_ref[...] = _silu(acc).astype(BF16)
        return
    n_heads = acc.shape[1] // hd
    is_qk = j < 2 * tiles_per_part
    qk_scale = jnp.where(j >= tiles_per_part, hd ** -0.5, 1.0)
    for t in range(n_heads):
        a = acc[:, t * hd:(t + 1) * hd]
        qk = a
        if rope:
            lane = lax.broadcasted_iota(jnp.int32, a.shape, 1)
            first = (lane % (hd // 2)) < (hd // 4)
            partner = jnp.where(first, pltpu.roll(a, hd - hd // 4, 1), pltpu.roll(a, hd // 4, 1))
            qk = a * cos_ref[...] + partner * sin_ref[...]
        o_ref[:, t * hd:(t + 1) * hd] = jnp.where(is_qk, qk * qk_scale, a).astype(BF16)


def _mm_ret_proj(x, w, layer, col0, n, hd, rope_tables, seq, gate, cast=False):
    m, k = x.shape
    part = n if gate else n // 3
    bm, bn = _tile(seq, 1024), _tile(part, 512 if cast else 1024)
    assert not cast or m == bm
    assert col0 % bn == 0
    j0 = col0 // bn
    rope = rope_tables is not None and not gate
    in_specs = [
        pl.BlockSpec((bm, k), lambda i, j: (i, 0)),
        pl.BlockSpec((None, k, bn), lambda i, j: (layer, 0, j0 + j)),
    ]
    args = [x, w]
    if rope:
        tiles_per_seq = seq // bm
        tab_spec = pl.BlockSpec((bm, hd), lambda i, j: (i % tiles_per_seq, 0))
        in_specs += [tab_spec, tab_spec]
        args += list(rope_tables)
    out_shape, out_specs = _cast_outputs(
        cast, jax.ShapeDtypeStruct((m, n), BF16), pl.BlockSpec((bm, bn), lambda i, j: (i, j)),
        [(1, k, n)], [pl.BlockSpec((None, k, bn), lambda i, j: (0, 0, j))])
    return pl.pallas_call(
        functools.partial(_mm_ret_proj_kernel, tiles_per_part=part // bn, hd=hd, rope=rope, gate=gate,
                          cast=cast),
        out_shape=out_shape,
        grid=(m // bm, n // bn),
        in_specs=in_specs,
        out_specs=out_specs,
        compiler_params=_params(2),
        name="mm_ret_g" if gate else "mm_ret_qkv",
    )(*args)


def _mm_swiglu_kernel(x_ref, wa_ref, wb_ref, *rest, tiles_per_expert, cast):
    if tiles_per_expert:
        comb_ref = rest[0]
        rest = rest[1:]
    o_ref = rest[0]
    x = x_ref[...]
    a = jnp.dot(x, _weight_tile(wa_ref, rest[1] if cast else None), preferred_element_type=F32)
    b = jnp.dot(x, _weight_tile(wb_ref, rest[2] if cast else None), preferred_element_type=F32)
    act = _silu(a) * b
    if tiles_per_expert:
        e = pl.program_id(1) // tiles_per_expert
        comb = comb_ref[...]
        lane = lax.broadcasted_iota(jnp.int32, comb.shape, 1)
        act = act * jnp.sum(jnp.where(lane == e, comb, 0.0), axis=-1, keepdims=True)
    o_ref[...] = act.astype(BF16)


def _mm_swiglu(x, wa, wb, layer, comb=None, cast=False):
    m, k = x.shape
    bm = _tile(m, 1024)
    assert not cast or (m == bm and wa is wb)
    moe = comb is not None
    f = wa.shape[-1] // 2 if wa is wb else wa.shape[-1]
    bn = _tile(f, 256 if cast else 512)
    nt = f // bn
    b_off = nt if wa is wb else 0
    if moe:
        n_e = wa.shape[1]
        wa_spec = pl.BlockSpec((None, None, k, bn), lambda i, j: (layer, j // nt, 0, j % nt))
        wb_spec = pl.BlockSpec((None, None, k, bn), lambda i, j: (layer, j // nt, 0, b_off + j % nt))
        w_shape = (1, n_e, k, f)
        wo_spec = pl.BlockSpec((None, None, k, bn), lambda i, j: (0, j // nt, 0, j % nt))
        n_out = n_e * f
    else:
        wa_spec = pl.BlockSpec((None, k, bn), lambda i, j: (layer, 0, j))
        wb_spec = pl.BlockSpec((None, k, bn), lambda i, j: (layer, 0, b_off + j))
        w_shape = (1, k, f)
        wo_spec = pl.BlockSpec((None, k, bn), lambda i, j: (0, 0, j))
        n_out = f
    in_specs = [pl.BlockSpec((bm, k), lambda i, j: (i, 0)), wa_spec, wb_spec]
    args = [x, wa, wb]
    if moe:
        in_specs.append(pl.BlockSpec((bm, LANES), lambda i, j: (i, 0)))
        args.append(comb)
    out_shape, out_specs = _cast_outputs(
        cast, jax.ShapeDtypeStruct((m, n_out), BF16), pl.BlockSpec((bm, bn), lambda i, j: (i, j)),
        [w_shape, w_shape], [wo_spec, wo_spec])
    return pl.pallas_call(
        functools.partial(_mm_swiglu_kernel, tiles_per_expert=nt if moe else 0, cast=cast),
        out_shape=out_shape,
        grid=(m // bm, n_out // bn),
        in_specs=in_specs,
        out_specs=out_specs,
        compiler_params=_params(2),
        name="mm_swiglu",
    )(*args)


def _mm_res_kernel(x_ref, w_ref, res_ref, gate_ref, o_ref, *wout):
    w = _weight_tile(w_ref, wout[0] if wout else None)
    acc = jnp.dot(x_ref[...], w, preferred_element_type=F32)
    o_ref[...] = res_ref[...] + gate_ref[...] * acc


def _mm_res(x, w, layer, res, gate, rows_per_batch, cast=False):
    m, k = x.shape
    n = w.shape[2]
    bm, bn = _tile(rows_per_batch, 1024), _tile(n, 512 if cast else 1024)
    assert not cast or m == bm
    tiles_per_batch = rows_per_batch // bm
    out_shape, out_specs = _cast_outputs(
        cast, jax.ShapeDtypeStruct((m, n), F32), pl.BlockSpec((bm, bn), lambda i, j: (i, j)),
        [(1, k, n)], [pl.BlockSpec((None, k, bn), lambda i, j: (0, 0, j))])
    return pl.pallas_call(
        _mm_res_kernel,
        out_shape=out_shape,
        grid=(m // bm, n // bn),
        in_specs=[
            pl.BlockSpec((bm, k), lambda i, j: (i, 0)),
            pl.BlockSpec((None, k, bn), lambda i, j: (layer, 0, j)),
            pl.BlockSpec((bm, bn), lambda i, j: (i, j)),
            pl.BlockSpec((None, 1, bn), lambda i, j: (i // tiles_per_batch, 0, j)),
        ],
        out_specs=out_specs,
        compiler_params=_params(2),
        name="mm_res",
    )(x, w, res, gate)


def _softmax_pv(scores, values):
    m = functools.reduce(jnp.maximum, [jnp.max(s, axis=-1, keepdims=True) for s in scores])
    out = functools.reduce(
        jnp.add,
        [jnp.dot(jnp.exp2(s - m).astype(BF16), v, preferred_element_type=F32)
         for s, v in zip(scores, values)])
    hd = out.shape[1] // 2
    return out[:, :hd] / out[:, hd:]


def _qk(q, k):
    return lax.dot_general(q, k, (((1,), (1,)), ((), ())), preferred_element_type=F32)


NA_Q_ROWS = 4
NA_KEY_ROWS = 12


def _na_block_geometry(t, rows, win_r):
    u0 = min(max(t * NA_Q_ROWS - win_r // 2, 0), rows - NA_KEY_ROWS)
    geo = []
    for a in range(NA_Q_ROWS):
        r = t * NA_Q_ROWS + a
        r0 = min(max(r - win_r // 2, 0), rows - win_r)
        geo.append([(r0 <= u0 + j < r0 + win_r, u0 + j - r + win_r - 1) for j in range(NA_KEY_ROWS)])
    return u0, geo


def _na_build_slabs(pairs_ref, slab_ref, rows, win_r):
    n_blk = rows // NA_Q_ROWS
    lane = lax.broadcasted_iota(jnp.int32, (GRID_W, 2 * GRID_W), 1)
    neg = jnp.full((GRID_W, 2 * GRID_W), NEG_INF, F32)
    for v, t in enumerate((0, 1, n_blk - 1)):
        _, geo = _na_block_geometry(t, rows, win_r)
        for a in range(NA_Q_ROWS):
            for jj in range(NA_KEY_ROWS // 2):
                (ok0, d0), (ok1, _) = geo[a][2 * jj], geo[a][2 * jj + 1]
                if ok0 and ok1:
                    tile = pairs_ref[d0 + 1]
                elif ok0:
                    tile = jnp.where(lane < GRID_W, pairs_ref[d0 + 1], neg)
                elif ok1:
                    tile = jnp.where(lane >= GRID_W, pairs_ref[d0 + 1], neg)
                else:
                    tile = neg
                slab_ref[v, a * GRID_W:(a + 1) * GRID_W, jj * 2 * GRID_W:(jj + 1) * 2 * GRID_W] = tile


def _na_attn_kernel(q_ref, k_ref, v_ref, qc_ref, kc_ref, vc_ref, pairs_ref, o_ref, oc_ref,
                    bias_ref, vaug_ref, vcaug_ref, *, rows, win_r):
    hd = v_ref.shape[1]

    @pl.when(pl.program_id(1) == 0)
    def _():
        _na_build_slabs(pairs_ref, bias_ref, rows, win_r)

    @pl.when((pl.program_id(0) == 0) & (pl.program_id(1) == 0))
    def _():
        vaug_ref[:, hd:] = jnp.ones((vaug_ref.shape[0], hd), BF16)
        vcaug_ref[:, hd:] = jnp.ones((vcaug_ref.shape[0], hd), BF16)

    vaug_ref[:, :hd] = v_ref[...]
    vcaug_ref[:, :hd] = vc_ref[...]
    kc = kc_ref[...]
    vc = vcaug_ref[...]
    n_blk = rows // NA_Q_ROWS
    n_q = NA_Q_ROWS * GRID_W
    n_k = NA_KEY_ROWS * GRID_W

    def query_block(t, carry):
        u0 = jnp.clip(t * NA_Q_ROWS - win_r // 2, 0, rows - NA_KEY_ROWS)
        slab = jnp.where(t == 0, 0, jnp.where(t == n_blk - 1, 2, 1))
        q_rows = pl.ds(pl.multiple_of(t * n_q, n_q), n_q)
        k_rows = pl.ds(pl.multiple_of(u0 * GRID_W, GRID_W), n_k)
        q = q_ref[q_rows, :]
        s_win = _qk(q, k_ref[k_rows, :]) + bias_ref[slab]
        o = _softmax_pv([s_win, _qk(q, kc)], [vaug_ref[k_rows, :], vc])
        o_ref[q_rows, :] = o.astype(BF16)
        return carry

    lax.fori_loop(0, n_blk, query_block, 0, unroll=16 if n_blk % 16 == 0 else 4)
    oc_ref[...] = _softmax_pv([_qk(qc_ref[...], kc)], [vc]).astype(BF16)


def _na_bias_pairs(rpb):
    n_heads, n_dr, n_dc = rpb.shape
    win_c = (n_dc + 1) // 2
    cols = jnp.arange(GRID_W)
    c0 = jnp.clip(cols - win_c // 2, 0, GRID_W - win_c)
    col_in = (cols[None, :] >= c0[:, None]) & (cols[None, :] < c0[:, None] + win_c)
    dc = jnp.clip(cols[None, :] - cols[:, None] + win_c - 1, 0, 2 * win_c - 2)
    masked = jnp.where(col_in, rpb[:, :, dc].astype(F32) * LOG2E, NEG_INF)
    neg = jnp.full((n_heads, 1, GRID_W, GRID_W), NEG_INF, F32)
    ext = jnp.concatenate([neg, masked, neg], axis=1)
    return jnp.concatenate([ext[:, :-1], ext[:, 1:]], axis=-1)


def _na_attention(qkv, qkvc, rpb, batch, seq, ctx_len, hd):
    n_heads = rpb.shape[0]
    win_r = (rpb.shape[1] + 1) // 2
    rows = seq // GRID_W
    n_blk = rows // NA_Q_ROWS
    assert rows % NA_Q_ROWS == 0 and n_blk >= 3 and NA_Q_ROWS >= win_r // 2
    assert (n_blk - 2) * NA_Q_ROWS - win_r // 2 <= rows - NA_KEY_ROWS
    assert NA_KEY_ROWS >= win_r + NA_Q_ROWS - 1 and NA_KEY_ROWS % 2 == 0
    d = n_heads * hd
    pairs = _na_bias_pairs(rpb)

    def spec(n_rows, part):
        return pl.BlockSpec((n_rows, hd), lambda h, b: (b, part * n_heads + h))

    return pl.pallas_call(
        functools.partial(_na_attn_kernel, rows=rows, win_r=win_r),
        out_shape=(jax.ShapeDtypeStruct((batch * seq, d), BF16),
                   jax.ShapeDtypeStruct((batch * ctx_len, d), BF16)),
        grid=(n_heads, batch),
        in_specs=[spec(seq, 0), spec(seq, 1), spec(seq, 2),
                  spec(ctx_len, 0), spec(ctx_len, 1), spec(ctx_len, 2),
                  pl.BlockSpec((None,) + pairs.shape[1:], lambda h, b: (h, 0, 0, 0))],
        out_specs=(pl.BlockSpec((seq, hd), lambda h, b: (b, h)),
                   pl.BlockSpec((ctx_len, hd), lambda h, b: (b, h))),
        scratch_shapes=[pltpu.VMEM((3, NA_Q_ROWS * GRID_W, NA_KEY_ROWS * GRID_W), F32),
                        pltpu.VMEM((seq, 2 * hd), BF16),
                        pltpu.VMEM((ctx_len, 2 * hd), BF16)],
        compiler_params=_params(2),
        name="na_attention",
    )(qkv, qkv, qkv, qkvc, qkvc, qkvc, pairs)


def _kv_outer(k, v):
    return lax.dot_general(k, v, (((0,), (0,)), ((), ())), preferred_element_type=F32)


def _retention_kernel(lgf_ref, lgb_ref, q_ref, k_ref, v_ref, g_ref, gnw_ref, *rest,
                      c_len, n_chunks, ctx_len):
    if ctx_len:
        kc_ref, vc_ref, o_ref, acc_ref, sf_ref, sb_ref, kvf_ref, kvb_ref = rest
    else:
        o_ref, acc_ref, sf_ref, sb_ref, kvf_ref, kvb_ref = rest
    h = pl.program_id(0)
    lgf = jnp.full((1, 1), lgf_ref[h], F32)
    lgb = jnp.full((1, 1), lgb_ref[h], F32)

    row = lax.broadcasted_iota(jnp.int32, (c_len, c_len), 0)
    col = lax.broadcasted_iota(jnp.int32, (c_len, c_len), 1)
    diff = (row - col).astype(F32)
    intra = jnp.where(diff >= 0, jnp.exp(lgf * jnp.maximum(diff, 0.0)),
                      jnp.exp(lgb * jnp.maximum(-diff, 0.0)))
    pos = lax.broadcasted_iota(jnp.int32, (c_len, 1), 0).astype(F32)
    q_dec_f = jnp.exp(lgf * (pos + 1.0))
    k_dec_f = jnp.exp(lgf * (c_len - 1.0 - pos))
    chunk_dec_f = jnp.exp(lgf * c_len)
    q_dec_b = jnp.exp(lgb * (c_len - pos))
    k_dec_b = jnp.exp(lgb * pos)
    chunk_dec_b = jnp.exp(lgb * c_len)

    def chunk_slice(c):
        if isinstance(c, int):
            return pl.ds(c * c_len, c_len)
        return pl.ds(pl.multiple_of(c * c_len, c_len), c_len)

    def kv_term(c, k_dec):
        sl = chunk_slice(c)
        return _kv_outer((k_ref[sl, :].astype(F32) * k_dec).astype(BF16), v_ref[sl, :])

    def fwd_value(c, state):
        sl = chunk_slice(c)
        q = q_ref[sl, :]
        s = (_qk(q, k_ref[sl, :]) * intra).astype(BF16)
        qf = (q.astype(F32) * q_dec_f).astype(BF16)
        return (jnp.dot(s, v_ref[sl, :], preferred_element_type=F32)
                + jnp.dot(qf, state.astype(BF16), preferred_element_type=F32))

    def bwd_value(c, state):
        qb = (q_ref[chunk_slice(c), :].astype(F32) * q_dec_b).astype(BF16)
        return jnp.dot(qb, state.astype(BF16), preferred_element_type=F32)

    group = 2 if n_chunks % 4 == 0 else 1

    def make_scan(s_ref, kv_ref, chunk_dec, k_dec, chunk_of, value_fn):
        def init(s_init):
            s_ref[...] = jnp.zeros_like(s_ref)
            kv_ref[0] = s_init
            for g in range(1, group):
                kv_ref[g] = kv_term(chunk_of(g - 1), k_dec)

        def step(p0):
            state = s_ref[...]
            vals = []
            for g in range(group):
                state = state * chunk_dec + kv_ref[g]
                vals.append(value_fn(chunk_of(p0 + g), state))
            s_ref[...] = state
            for g in range(group):
                nxt = jnp.minimum(p0 + group - 1 + g, n_chunks - 1)
                kv_ref[g] = kv_term(chunk_of(nxt), k_dec)
            return vals

        return init, step

    fwd_init, fwd_step = make_scan(sf_ref, kvf_ref, chunk_dec_f, k_dec_f, lambda p: p, fwd_value)
    bwd_init, bwd_step = make_scan(sb_ref, kvb_ref, chunk_dec_b, k_dec_b,
                                   lambda p: n_chunks - 1 - p, bwd_value)

    if ctx_len:
        cpos = lax.broadcasted_iota(jnp.int32, (ctx_len, 1), 0).astype(F32)
        kc = kc_ref[...].astype(F32)
        vc = vc_ref[...]
        fwd_init(_kv_outer((kc * jnp.exp(lgf * (ctx_len - 1.0 - cpos))).astype(BF16), vc))
        bwd_init(_kv_outer((kc * jnp.exp(lgb * cpos)).astype(BF16), vc))
    else:
        zero = jnp.zeros(sf_ref.shape, F32)
        fwd_init(zero)
        bwd_init(zero)

    def finish(c, o):
        sl = chunk_slice(c)
        mu = jnp.mean(o, axis=-1, keepdims=True)
        cen = o - mu
        var = jnp.mean(cen * cen, axis=-1, keepdims=True)
        normed = cen * lax.rsqrt(var + EPS)
        o_ref[sl, :] = (normed * gnw_ref[...] * g_ref[sl, :].astype(F32)).astype(BF16)

    half = n_chunks // 2

    def first_half(i, carry):
        p0 = i * group
        for g, (fv, bv) in enumerate(zip(fwd_step(p0), bwd_step(p0))):
            acc_ref[chunk_slice(p0 + g), :] = fv
            acc_ref[chunk_slice(n_chunks - 1 - p0 - g), :] = bv
        return carry

    def second_half(i, carry):
        p0 = half + i * group
        for g, (fv, bv) in enumerate(zip(fwd_step(p0), bwd_step(p0))):
            lo, hi = n_chunks - 1 - p0 - g, p0 + g
            finish(hi, acc_ref[chunk_slice(hi), :] + fv)
            finish(lo, acc_ref[chunk_slice(lo), :] + bv)
        return carry

    lax.fori_loop(0, half // group, first_half, 0, unroll=True)
    lax.fori_loop(0, half // group, second_half, 0, unroll=True)


def _retention(qkv, gate, lg_f, lg_b, gn_w, batch, n_tok, hd, qkv_ctx=None, ctx_len=0):
    n_heads = lg_f.shape[0]
    d = n_heads * hd
    c_len = 2 * RET_CHUNK if n_tok % (8 * RET_CHUNK) == 0 else RET_CHUNK
    n_chunks = n_tok // c_len
    assert n_tok % (2 * c_len) == 0

    def spec(n_rows, part):
        return pl.BlockSpec((n_rows, hd), lambda h, b: (b, part * n_heads + h))

    smem = pl.BlockSpec(memory_space=pltpu.SMEM)
    in_specs = [smem, smem, spec(n_tok, 0), spec(n_tok, 1), spec(n_tok, 2), spec(n_tok, 0),
                pl.BlockSpec((1, hd), lambda h, b: (0, h))]
    args = [lg_f, lg_b, qkv, qkv, qkv, gate, gn_w.reshape(1, d)]
    if ctx_len:
        in_specs += [spec(ctx_len, 1), spec(ctx_len, 2)]
        args += [qkv_ctx, qkv_ctx]
    return pl.pallas_call(
        functools.partial(_retention_kernel, c_len=c_len, n_chunks=n_chunks, ctx_len=ctx_len),
        out_shape=jax.ShapeDtypeStruct((batch * n_tok, d), BF16),
        grid=(n_heads, batch),
        in_specs=in_specs,
        out_specs=pl.BlockSpec((n_tok, hd), lambda h, b: (b, h)),
        scratch_shapes=[pltpu.VMEM((n_tok, hd), F32),
                        pltpu.VMEM((hd, hd), F32),
                        pltpu.VMEM((hd, hd), F32),
                        pltpu.VMEM((2, hd, hd), F32),
                        pltpu.VMEM((2, hd, hd), F32)],
        compiler_params=_params(2),
        name="retention",
    )(*args)


def _rope_tables(seq, hd):
    t = jnp.arange(seq)
    row = (t // GRID_W).astype(F32)
    col = (t % GRID_W).astype(F32)
    axis_dim = hd // 2
    inv_freq = jnp.power(ROPE_BASE, -jnp.arange(0, axis_dim, 2, dtype=F32) / axis_dim)
    ang_r = row[:, None] * inv_freq[None, :]
    ang_c = col[:, None] * inv_freq[None, :]
    cos = jnp.concatenate([jnp.cos(ang_r), jnp.cos(ang_r), jnp.cos(ang_c), jnp.cos(ang_c)], axis=-1)
    sin = jnp.concatenate([-jnp.sin(ang_r), jnp.sin(ang_r), -jnp.sin(ang_c), jnp.sin(ang_c)], axis=-1)
    return cos, sin


def kernel(x, c, ctx, c_ctx, ada_w, ada_b, norm1_w, norm2_w, na_wqkv, na_wo, na_qnorm_w, na_knorm_w, na_rpb, ret_wqkvg, ret_wo, ret_decay_f, ret_decay_b, ret_gn_w, ffn_w13, ffn_w2, moe_router, moe_w13, moe_w2):
    batch, seq, d = x.shape
    ctx_len = ctx.shape[1]
    n_ctx = batch * ctx_len
    depth = ada_w.shape[0]
    na_hd = na_qnorm_w.shape[1]
    ret_heads = ret_decay_f.shape[1]
    ret_hd = d // ret_heads
    assert batch + 1 <= ADA_ROWS

    cond = jnp.zeros((ADA_ROWS, d), F32).at[:batch].set(c).at[batch].set(c_ctx)
    ada = _ada_params(cond, ada_w, ada_b)
    rope = _rope_tables(seq, ret_hd)
    moe_w2_flat = moe_w2.reshape(moe_w2.shape[0], -1, d)

    h = x.reshape(batch * seq, d)
    hc = ctx.reshape(n_ctx, d)
    for i in range(depth):
        j = i // 2
        ctx_out = i < depth - 1
        mods = ada[i].reshape(ADA_ROWS, 6, 1, d)
        sh1, sc1, g1, sh2, sc2, g2 = (mods[:batch, p] for p in range(6))
        csh1, csc1, cg1, csh2, csc2, cg2 = (mods[batch:batch + 1, p] for p in range(6))

        def bf16_layer(w):
            return w[j:j + 1].astype(BF16)

        u = _norm_mod(h, norm1_w[i], sh1, sc1, seq)
        uc = _norm_mod(hc, norm1_w[i], csh1, csc1, n_ctx)
        if i % 2 == 0:
            w_o = na_wo
            qkvc, w_qkv = _mm_na_qkv(uc, na_wqkv, j, na_qnorm_w[j], na_knorm_w[j], cast=True)
            qkv = _mm_na_qkv(u, w_qkv, 0, na_qnorm_w[j], na_knorm_w[j])
            o, oc = _na_attention(qkv, qkvc, na_rpb[j], batch, seq, ctx_len, na_hd)
        else:
            w_o = ret_wo
            lg_f = jax.nn.log_sigmoid(ret_decay_f[j].astype(F32))
            lg_b = jax.nn.log_sigmoid(ret_decay_b[j].astype(F32))
            qkvc, w_qkv = _mm_ret_proj(uc, ret_wqkvg, j, 0, 3 * d, ret_hd, None, n_ctx, False, cast=True)
            gc, w_g = _mm_ret_proj(uc, ret_wqkvg, j, 3 * d, d, ret_hd, None, n_ctx, True, cast=True)
            qkv = _mm_ret_proj(u, w_qkv, 0, 0, 3 * d, ret_hd, rope, seq, False)
            g = _mm_ret_proj(u, w_g, 0, 0, d, ret_hd, None, seq, True)
            o = _retention(qkv, g, lg_f, lg_b, ret_gn_w[j], batch, seq, ret_hd, qkvc, ctx_len)
            if ctx_out:
                oc = _retention(qkvc, gc, lg_f, lg_b, ret_gn_w[j], batch, ctx_len, ret_hd)
        if ctx_out:
            hc, w_o_b = _mm_res(oc, w_o, j, hc, cg1, n_ctx, cast=True)
        else:
            w_o_b = bf16_layer(w_o)
        h = _mm_res(o, w_o_b, 0, h, g1, seq)

        if i % 2 == 0:
            w13, w2, router = ffn_w13, ffn_w2, None
        else:
            w13, w2, router = moe_w13, moe_w2_flat, moe_router[j]
        comb = combc = None
        if ctx_out:
            uc2 = _norm_mod(hc, norm2_w[i], csh2, csc2, n_ctx, router)
            if router is not None:
                uc2, combc = uc2
            actc, w_a, w_b = _mm_swiglu(uc2, w13, w13, j, combc, cast=True)
            hc, w2_b = _mm_res(actc, w2, j, hc, cg2, n_ctx, cast=True)
        else:
            w_a = w_b = bf16_layer(w13)
            w2_b = bf16_layer(w2)
        u2 = _norm_mod(h, norm2_w[i], sh2, sc2, seq, router)
        if router is not None:
            u2, comb = u2
        h = _mm_res(_mm_swiglu(u2, w_a, w_b, 0, comb), w2_b, 0, h, g2, seq)
    return h.reshape(batch, seq, d)
```

```python
import functools

import jax
import jax.numpy as jnp
from jax import lax
from jax.experimental import pallas as pl
from jax.experimental.pallas import tpu as pltpu

F32 = jnp.float32
BF16 = jnp.bfloat16

GRID_W = 64
RET_CHUNK = 128
ROPE_BASE = 10000.0
EPS = 1e-6
NEG_INF = -1e30
LOG2E = 1.4426950408889634

V7X_VMEM_BYTES = 64 * 1024 * 1024
VMEM_LIMIT_BYTES = V7X_VMEM_BYTES * 7 // 8
LANES = 128
ADA_ROWS = 8


def _params(n_grid_dims):
    return pltpu.CompilerParams(
        dimension_semantics=("arbitrary",) * n_grid_dims,
        vmem_limit_bytes=VMEM_LIMIT_BYTES,
    )


def _tile(n, pref):
    t = min(pref, n)
    while n % t:
        t //= 2
    return t


def _silu(x):
    return x * (1.0 / (1.0 + jnp.exp(-x)))


def _ada_kernel(x_ref, w_ref, b_ref, o_ref):
    xa = _silu(x_ref[...]).astype(BF16)
    w = w_ref[...].astype(BF16)
    o_ref[...] = jnp.dot(xa, w, preferred_element_type=F32) + b_ref[...]


def _ada_params(cond, ada_w, ada_b):
    depth, d, n = ada_w.shape
    bn = _tile(n, 512)
    return pl.pallas_call(
        _ada_kernel,
        out_shape=jax.ShapeDtypeStruct((depth, ADA_ROWS, n), F32),
        grid=(depth, n // bn),
        in_specs=[
            pl.BlockSpec((ADA_ROWS, d), lambda l, j: (0, 0)),
            pl.BlockSpec((None, d, bn), lambda l, j: (l, 0, j)),
            pl.BlockSpec((None, 1, bn), lambda l, j: (l, 0, j)),
        ],
        out_specs=pl.BlockSpec((None, ADA_ROWS, bn), lambda l, j: (l, 0, j)),
        compiler_params=_params(2),
        name="ada_params",
    )(cond, ada_w, ada_b.reshape(depth, 1, n))


SUBLANES = 8
NORM_ROWS = 16


def _norm_mod_kernel(x_ref, nw_ref, sh_ref, sc_ref, *rest, n_experts):
    if n_experts == 0:
        u_ref, gain_ref, shift_ref = rest
    else:
        wr_ref, u_ref, comb_ref, gain_ref, shift_ref = rest
    d = x_ref.shape[1]
    gain_ref[...] = jnp.broadcast_to(nw_ref[...] * (1.0 + sc_ref[...]), (SUBLANES, d))
    shift_ref[...] = jnp.broadcast_to(sh_ref[...], (SUBLANES, d))

    def row_group(r, carry):
        halves = []
        for s in range(NORM_ROWS // SUBLANES):
            rows = pl.ds(pl.multiple_of(r * NORM_ROWS + s * SUBLANES, SUBLANES), SUBLANES)
            x = x_ref[rows, :]
            inv_rms = lax.rsqrt(jnp.mean(x * x, axis=-1, keepdims=True) + EPS)
            halves.append((x_ref[rows, :] * inv_rms) * gain_ref[...] + shift_ref[...])
        rows = pl.ds(pl.multiple_of(r * NORM_ROWS, NORM_ROWS), NORM_ROWS)
        u_ref[rows, :] = jnp.concatenate(halves, axis=0).astype(BF16)
        return carry

    lax.fori_loop(0, x_ref.shape[0] // NORM_ROWS, row_group, 0, unroll=4)
    if n_experts == 0:
        return
    logits = jnp.dot(u_ref[...], wr_ref[...], preferred_element_type=F32)
    lane = lax.broadcasted_iota(jnp.int32, logits.shape, 1).astype(F32)
    valid = lane < n_experts
    logits = jnp.where(valid, logits, NEG_INF)
    e = jnp.exp(logits - jnp.max(logits, axis=-1, keepdims=True))
    probs = jnp.where(valid, e / jnp.sum(e, axis=-1, keepdims=True), -1.0)
    p1 = jnp.max(probs, axis=-1, keepdims=True)
    i1 = jnp.min(jnp.where(probs == p1, lane, float(LANES)), axis=-1, keepdims=True)
    rest_p = jnp.where(lane == i1, -1.0, probs)
    p2 = jnp.max(rest_p, axis=-1, keepdims=True)
    i2 = jnp.min(jnp.where(rest_p == p2, lane, float(LANES)), axis=-1, keepdims=True)
    inv = 1.0 / (p1 + p2)
    comb_ref[...] = jnp.where(lane == i1, p1 * inv, jnp.where(lane == i2, p2 * inv, 0.0))


def _norm_mod(x, norm_w, shift, scale, rows_per_batch, router_w=None):
    m, d = x.shape
    bm = _tile(rows_per_batch, min(1024, max(16, m // 8)))
    tiles_per_batch = rows_per_batch // bm
    mod_spec = pl.BlockSpec((None, 1, d), lambda i: (i // tiles_per_batch, 0, 0))
    in_specs = [
        pl.BlockSpec((bm, d), lambda i: (i, 0)),
        pl.BlockSpec((1, d), lambda i: (0, 0)),
        mod_spec,
        mod_spec,
    ]
    args = [x, norm_w.reshape(1, d), shift, scale]
    u_shape = jax.ShapeDtypeStruct((m, d), BF16)
    u_spec = pl.BlockSpec((bm, d), lambda i: (i, 0))
    if router_w is None:
        n_experts = 0
        out_shape, out_specs = u_shape, u_spec
    else:
        n_experts = router_w.shape[1]
        wr = jnp.zeros((d, LANES), BF16).at[:, :n_experts].set(router_w.astype(BF16))
        in_specs.append(pl.BlockSpec((d, LANES), lambda i: (0, 0)))
        args.append(wr)
        out_shape = (u_shape, jax.ShapeDtypeStruct((m, LANES), F32))
        out_specs = (u_spec, pl.BlockSpec((bm, LANES), lambda i: (i, 0)))
    return pl.pallas_call(
        functools.partial(_norm_mod_kernel, n_experts=n_experts),
        out_shape=out_shape,
        grid=(m // bm,),
        in_specs=in_specs,
        out_specs=out_specs,
        scratch_shapes=[pltpu.VMEM((SUBLANES, d), F32), pltpu.VMEM((SUBLANES, d), F32)],
        compiler_params=_params(1),
        name="norm_mod",
    )(*args)


def _weight_tile(w_ref, wout_ref):
    if wout_ref is None:
        return w_ref[...]
    w = w_ref[...].astype(BF16)
    wout_ref[...] = w
    return w


def _mm_na_qkv_kernel(x_ref, w_ref, qn_ref, kn_ref, o_ref, *wout, tiles_per_part, hd):
    j = pl.program_id(1)
    w = _weight_tile(w_ref, wout[0] if wout else None)
    acc = jnp.dot(x_ref[...], w, preferred_element_type=F32)
    n_heads = acc.shape[1] // hd

    is_q = j < tiles_per_part
    is_v = j >= 2 * tiles_per_part
    nw = jnp.where(is_q, qn_ref[...] * (hd ** -0.5 * LOG2E), kn_ref[...])
    for t in range(n_heads):
        a = acc[:, t * hd:(t + 1) * hd]
        y = a * lax.rsqrt(jnp.mean(a * a, axis=-1, keepdims=True) + EPS)
        o_ref[:, t * hd:(t + 1) * hd] = jnp.where(is_v, a, y * nw).astype(BF16)


def _cast_outputs(cast, out_shape, out_spec, w_shapes, w_specs):
    if not cast:
        return out_shape, out_spec
    return ((out_shape,) + tuple(jax.ShapeDtypeStruct(s, BF16) for s in w_shapes),
            (out_spec,) + tuple(w_specs))


def _mm_na_qkv(x, w, layer, qn_w, kn_w, cast=False):
    m, k = x.shape
    n = w.shape[2]
    hd = qn_w.shape[0]
    bm, bn = _tile(m, 1024), _tile(n // 3, 512 if cast else 1024)
    assert not cast or m == bm
    out_shape, out_specs = _cast_outputs(
        cast, jax.ShapeDtypeStruct((m, n), BF16), pl.BlockSpec((bm, bn), lambda i, j: (i, j)),
        [(1, k, n)], [pl.BlockSpec((None, k, bn), lambda i, j: (0, 0, j))])
    return pl.pallas_call(
        functools.partial(_mm_na_qkv_kernel, tiles_per_part=n // 3 // bn, hd=hd),
        out_shape=out_shape,
        grid=(m // bm, n // bn),
        in_specs=[
            pl.BlockSpec((bm, k), lambda i, j: (i, 0)),
            pl.BlockSpec((None, k, bn), lambda i, j: (layer, 0, j)),
            pl.BlockSpec((1, hd), lambda i, j: (0, 0)),
            pl.BlockSpec((1, hd), lambda i, j: (0, 0)),
        ],
        out_specs=out_specs,
        compiler_params=_params(2),
        name="mm_na_qkv",
    )(x, w, qn_w.reshape(1, hd), kn_w.reshape(1, hd))


def _mm_ret_proj_kernel(x_ref, w_ref, *rest, tiles_per_part, hd, rope, gate, cast):
    if rope:
        cos_ref, sin_ref = rest[:2]
        rest = rest[2:]
    o_ref = rest[0]
    j = pl.program_id(1)
    w = _weight_tile(w_ref, rest[1] if cast else None)
    acc = jnp.dot(x_ref[...], w, preferred_element_type=F32)
    if gate:
        o_ref[...] = _silu(acc).astype(BF16)
        return
    n_heads = acc.shape[1] // hd
    is_qk = j < 2 * tiles_per_part
    qk_scale = jnp.where(j >= tiles_per_part, hd ** -0.5, 1.0)
    for t in range(n_heads):
        a = acc[:, t * hd:(t + 1) * hd]
        qk = a
        if rope:
            lane = lax.broadcasted_iota(jnp.int32, a.shape, 1)
            first = (lane % (hd // 2)) < (hd // 4)
            partner = jnp.where(first, pltpu.roll(a, hd - hd // 4, 1), pltpu.roll(a, hd // 4, 1))
            qk = a * cos_ref[...] + partner * sin_ref[...]
        o_ref[:, t * hd:(t + 1) * hd] = jnp.where(is_qk, qk * qk_scale, a).astype(BF16)


def _mm_ret_proj(x, w, layer, col0, n, hd, rope_tables, seq, gate, cast=False):
    m, k = x.shape
    part = n if gate else n // 3
    bm, bn = _tile(seq, 1024), _tile(part, 512 if cast else 1024)
    assert not cast or m == bm
    assert col0 % bn == 0
    j0 = col0 // bn
    rope = rope_tables is not None and not gate
    in_specs = [
        pl.BlockSpec((bm, k), lambda i, j: (i, 0)),
        pl.BlockSpec((None, k, bn), lambda i, j: (layer, 0, j0 + j)),
    ]
    args = [x, w]
    if rope:
        tiles_per_seq = seq // bm
        tab_spec = pl.BlockSpec((bm, hd), lambda i, j: (i % tiles_per_seq, 0))
        in_specs += [tab_spec, tab_spec]
        args += list(rope_tables)
    out_shape, out_specs = _cast_outputs(
        cast, jax.ShapeDtypeStruct((m, n), BF16), pl.BlockSpec((bm, bn), lambda i, j: (i, j)),
        [(1, k, n)], [pl.BlockSpec((None, k, bn), lambda i, j: (0, 0, j))])
    return pl.pallas_call(
        functools.partial(_mm_ret_proj_kernel, tiles_per_part=part // bn, hd=hd, rope=rope, gate=gate,
                          cast=cast),
        out_shape=out_shape,
        grid=(m // bm, n // bn),
        in_specs=in_specs,
        out_specs=out_specs,
        compiler_params=_params(2),
        name="mm_ret_g" if gate else "mm_ret_qkv",
    )(*args)


def _mm_swiglu_kernel(x_ref, wa_ref, wb_ref, *rest, tiles_per_expert, cast):
    if tiles_per_expert:
        comb_ref = rest[0]
        rest = rest[1:]
    o_ref = rest[0]
    x = x_ref[...]
    a = jnp.dot(x, _weight_tile(wa_ref, rest[1] if cast else None), preferred_element_type=F32)
    b = jnp.dot(x, _weight_tile(wb_ref, rest[2] if cast else None), preferred_element_type=F32)
    act = _silu(a) * b
    if tiles_per_expert:
        e = pl.program_id(1) // tiles_per_expert
        comb = comb_ref[...]
        lane = lax.broadcasted_iota(jnp.int32, comb.shape, 1)
        act = act * jnp.sum(jnp.where(lane == e, comb, 0.0), axis=-1, keepdims=True)
    o_ref[...] = act.astype(BF16)


def _mm_swiglu(x, wa, wb, layer, comb=None, cast=False):
    m, k = x.shape
    bm = _tile(m, 1024)
    assert not cast or (m == bm and wa is wb)
    moe = comb is not None
    f = wa.shape[-1] // 2 if wa is wb else wa.shape[-1]
    bn = _tile(f, 256 if cast else 512)
    nt = f // bn
    b_off = nt if wa is wb else 0
    if moe:
        n_e = wa.shape[1]
        wa_spec = pl.BlockSpec((None, None, k, bn), lambda i, j: (layer, j // nt, 0, j % nt))
        wb_spec = pl.BlockSpec((None, None, k, bn), lambda i, j: (layer, j // nt, 0, b_off + j % nt))
        w_shape = (1, n_e, k, f)
        wo_spec = pl.BlockSpec((None, None, k, bn), lambda i, j: (0, j // nt, 0, j % nt))
        n_out = n_e * f
    else:
        wa_spec = pl.BlockSpec((None, k, bn), lambda i, j: (layer, 0, j))
        wb_spec = pl.BlockSpec((None, k, bn), lambda i, j: (layer, 0, b_off + j))
        w_shape = (1, k, f)
        wo_spec = pl.BlockSpec((None, k, bn), lambda i, j: (0, 0, j))
        n_out = f
    in_specs = [pl.BlockSpec((bm, k), lambda i, j: (i, 0)), wa_spec, wb_spec]
    args = [x, wa, wb]
    if moe:
        in_specs.append(pl.BlockSpec((bm, LANES), lambda i, j: (i, 0)))
        args.append(comb)
    out_shape, out_specs = _cast_outputs(
        cast, jax.ShapeDtypeStruct((m, n_out), BF16), pl.BlockSpec((bm, bn), lambda i, j: (i, j)),
        [w_shape, w_shape], [wo_spec, wo_spec])
    return pl.pallas_call(
        functools.partial(_mm_swiglu_kernel, tiles_per_expert=nt if moe else 0, cast=cast),
        out_shape=out_shape,
        grid=(m // bm, n_out // bn),
        in_specs=in_specs,
        out_specs=out_specs,
        compiler_params=_params(2),
        name="mm_swiglu",
    )(*args)


def _mm_res_kernel(x_ref, w_ref, res_ref, gate_ref, o_ref, *wout):
    w = _weight_tile(w_ref, wout[0] if wout else None)
    acc = jnp.dot(x_ref[...], w, preferred_element_type=F32)
    o_ref[...] = res_ref[...] + gate_ref[...] * acc


def _mm_res(x, w, layer, res, gate, rows_per_batch, cast=False):
    m, k = x.shape
    n = w.shape[2]
    bm, bn = _tile(rows_per_batch, 1024), _tile(n, 512 if cast else 1024)
    assert not cast or m == bm
    tiles_per_batch = rows_per_batch // bm
    out_shape, out_specs = _cast_outputs(
        cast, jax.ShapeDtypeStruct((m, n), F32), pl.BlockSpec((bm, bn), lambda i, j: (i, j)),
        [(1, k, n)], [pl.BlockSpec((None, k, bn), lambda i, j: (0, 0, j))])
    return pl.pallas_call(
        _mm_res_kernel,
        out_shape=out_shape,
        grid=(m // bm, n // bn),
        in_specs=[
            pl.BlockSpec((bm, k), lambda i, j: (i, 0)),
            pl.BlockSpec((None, k, bn), lambda i, j: (layer, 0, j)),
            pl.BlockSpec((bm, bn), lambda i, j: (i, j)),
            pl.BlockSpec((None, 1, bn), lambda i, j: (i // tiles_per_batch, 0, j)),
        ],
        out_specs=out_specs,
        compiler_params=_params(2),
        name="mm_res",
    )(x, w, res, gate)


def _softmax_pv(scores, values):
    m = functools.reduce(jnp.maximum, [jnp.max(s, axis=-1, keepdims=True) for s in scores])
    out = functools.reduce(
        jnp.add,
        [jnp.dot(jnp.exp2(s - m).astype(BF16), v, preferred_element_type=F32)
         for s, v in zip(scores, values)])
    hd = out.shape[1] // 2
    return out[:, :hd] / out[:, hd:]


def _qk(q, k):
    return lax.dot_general(q, k, (((1,), (1,)), ((), ())), preferred_element_type=F32)


NA_Q_ROWS = 4
NA_KEY_ROWS = 12


def _na_block_geometry(t, rows, win_r):
    u0 = min(max(t * NA_Q_ROWS - win_r // 2, 0), rows - NA_KEY_ROWS)
    geo = []
    for a in range(NA_Q_ROWS):
        r = t * NA_Q_ROWS + a
        r0 = min(max(r - win_r // 2, 0), rows - win_r)
        geo.append([(r0 <= u0 + j < r0 + win_r, u0 + j - r + win_r - 1) for j in range(NA_KEY_ROWS)])
    return u0, geo


def _na_build_slabs(pairs_ref, slab_ref, rows, win_r):
    n_blk = rows // NA_Q_ROWS
    lane = lax.broadcasted_iota(jnp.int32, (GRID_W, 2 * GRID_W), 1)
    neg = jnp.full((GRID_W, 2 * GRID_W), NEG_INF, F32)
    for v, t in enumerate((0, 1, n_blk - 1)):
        _, geo = _na_block_geometry(t, rows, win_r)
        for a in range(NA_Q_ROWS):
            for jj in range(NA_KEY_ROWS // 2):
                (ok0, d0), (ok1, _) = geo[a][2 * jj], geo[a][2 * jj + 1]
                if ok0 and ok1:
                    tile = pairs_ref[d0 + 1]
                elif ok0:
                    tile = jnp.where(lane < GRID_W, pairs_ref[d0 + 1], neg)
                elif ok1:
                    tile = jnp.where(lane >= GRID_W, pairs_ref[d0 + 1], neg)
                else:
                    tile = neg
                slab_ref[v, a * GRID_W:(a + 1) * GRID_W, jj * 2 * GRID_W:(jj + 1) * 2 * GRID_W] = tile


def _na_attn_kernel(q_ref, k_ref, v_ref, qc_ref, kc_ref, vc_ref, pairs_ref, o_ref, oc_ref,
                    bias_ref, vaug_ref, vcaug_ref, *, rows, win_r):
    hd = v_ref.shape[1]

    @pl.when(pl.program_id(1) == 0)
    def _():
        _na_build_slabs(pairs_ref, bias_ref, rows, win_r)

    @pl.when((pl.program_id(0) == 0) & (pl.program_id(1) == 0))
    def _():
        vaug_ref[:, hd:] = jnp.ones((vaug_ref.shape[0], hd), BF16)
        vcaug_ref[:, hd:] = jnp.ones((vcaug_ref.shape[0], hd), BF16)

    vaug_ref[:, :hd] = v_ref[...]
    vcaug_ref[:, :hd] = vc_ref[...]
    kc = kc_ref[...]
    vc = vcaug_ref[...]
    n_blk = rows // NA_Q_ROWS
    n_q = NA_Q_ROWS * GRID_W
    n_k = NA_KEY_ROWS * GRID_W

    def query_block(t, carry):
        u0 = jnp.clip(t * NA_Q_ROWS - win_r // 2, 0, rows - NA_KEY_ROWS)
        slab = jnp.where(t == 0, 0, jnp.where(t == n_blk - 1, 2, 1))
        q_rows = pl.ds(pl.multiple_of(t * n_q, n_q), n_q)
        k_rows = pl.ds(pl.multiple_of(u0 * GRID_W, GRID_W), n_k)
        q = q_ref[q_rows, :]
        s_win = _qk(q, k_ref[k_rows, :]) + bias_ref[slab]
        o = _softmax_pv([s_win, _qk(q, kc)], [vaug_ref[k_rows, :], vc])
        o_ref[q_rows, :] = o.astype(BF16)
        return carry

    lax.fori_loop(0, n_blk, query_block, 0, unroll=16 if n_blk % 16 == 0 else 4)
    oc_ref[...] = _softmax_pv([_qk(qc_ref[...], kc)], [vc]).astype(BF16)


def _na_bias_pairs(rpb):
    n_heads, n_dr, n_dc = rpb.shape
    win_c = (n_dc + 1) // 2
    cols = jnp.arange(GRID_W)
    c0 = jnp.clip(cols - win_c // 2, 0, GRID_W - win_c)
    col_in = (cols[None, :] >= c0[:, None]) & (cols[None, :] < c0[:, None] + win_c)
    dc = jnp.clip(cols[None, :] - cols[:, None] + win_c - 1, 0, 2 * win_c - 2)
    pick = (jnp.arange(n_dc)[:, None] == dc.reshape(1, -1)).astype(F32)
    bias = jnp.dot(rpb.astype(F32).reshape(n_heads * n_dr, n_dc), pick, precision=lax.Precision.HIGHEST)
    bias = bias.reshape(n_heads, n_dr, GRID_W, GRID_W)
    masked = jnp.where(col_in, bias * LOG2E, NEG_INF)
    neg = jnp.full((n_heads, 1, GRID_W, GRID_W), NEG_INF, F32)
    ext = jnp.concatenate([neg, masked, neg], axis=1)
    return jnp.concatenate([ext[:, :-1], ext[:, 1:]], axis=-1)


def _na_attention(qkv, qkvc, rpb, batch, seq, ctx_len, hd):
    n_heads = rpb.shape[0]
    win_r = (rpb.shape[1] + 1) // 2
    rows = seq // GRID_W
    n_blk = rows // NA_Q_ROWS
    assert rows % NA_Q_ROWS == 0 and n_blk >= 3 and NA_Q_ROWS >= win_r // 2
    assert (n_blk - 2) * NA_Q_ROWS - win_r // 2 <= rows - NA_KEY_ROWS
    assert NA_KEY_ROWS >= win_r + NA_Q_ROWS - 1 and NA_KEY_ROWS % 2 == 0
    d = n_heads * hd
    pairs = _na_bias_pairs(rpb)

    def spec(n_rows, part):
        return pl.BlockSpec((n_rows, hd), lambda h, b: (b, part * n_heads + h))

    return pl.pallas_call(
        functools.partial(_na_attn_kernel, rows=rows, win_r=win_r),
        out_shape=(jax.ShapeDtypeStruct((batch * seq, d), BF16),
                   jax.ShapeDtypeStruct((batch * ctx_len, d), BF16)),
        grid=(n_heads, batch),
        in_specs=[spec(seq, 0), spec(seq, 1), spec(seq, 2),
                  spec(ctx_len, 0), spec(ctx_len, 1), spec(ctx_len, 2),
                  pl.BlockSpec((None,) + pairs.shape[1:], lambda h, b: (h, 0, 0, 0))],
        out_specs=(pl.BlockSpec((seq, hd), lambda h, b: (b, h)),
                   pl.BlockSpec((ctx_len, hd), lambda h, b: (b, h))),
        scratch_shapes=[pltpu.VMEM((3, NA_Q_ROWS * GRID_W, NA_KEY_ROWS * GRID_W), F32),
                        pltpu.VMEM((seq, 2 * hd), BF16),
                        pltpu.VMEM((ctx_len, 2 * hd), BF16)],
        compiler_params=_params(2),
        name="na_attention",
    )(qkv, qkv, qkv, qkvc, qkvc, qkvc, pairs)


def _kv_outer(k, v):
    return lax.dot_general(k, v, (((0,), (0,)), ((), ())), preferred_element_type=F32)


def _retention_kernel(lgf_ref, lgb_ref, q_ref, k_ref, v_ref, g_ref, gnw_ref, *rest,
                      c_len, n_chunks, ctx_len):
    if ctx_len:
        kc_ref, vc_ref, o_ref, acc_ref, sf_ref, sb_ref, kvf_ref, kvb_ref = rest
    else:
        o_ref, acc_ref, sf_ref, sb_ref, kvf_ref, kvb_ref = rest
    h = pl.program_id(0)
    lgf = jnp.full((1, 1), lgf_ref[h], F32)
    lgb = jnp.full((1, 1), lgb_ref[h], F32)

    row = lax.broadcasted_iota(jnp.int32, (c_len, c_len), 0)
    col = lax.broadcasted_iota(jnp.int32, (c_len, c_len), 1)
    diff = (row - col).astype(F32)
    intra = jnp.where(diff >= 0, jnp.exp(lgf * jnp.maximum(diff, 0.0)),
                      jnp.exp(lgb * jnp.maximum(-diff, 0.0)))
    pos = lax.broadcasted_iota(jnp.int32, (c_len, 1), 0).astype(F32)
    q_dec_f = jnp.exp(lgf * (pos + 1.0))
    k_dec_f = jnp.exp(lgf * (c_len - 1.0 - pos))
    chunk_dec_f = jnp.exp(lgf * c_len)
    q_dec_b = jnp.exp(lgb * (c_len - pos))
    k_dec_b = jnp.exp(lgb * pos)
    chunk_dec_b = jnp.exp(lgb * c_len)

    def chunk_slice(c):
        if isinstance(c, int):
            return pl.ds(c * c_len, c_len)
        return pl.ds(pl.multiple_of(c * c_len, c_len), c_len)

    def kv_term(c, k_dec):
        sl = chunk_slice(c)
        return _kv_outer((k_ref[sl, :].astype(F32) * k_dec).astype(BF16), v_ref[sl, :])

    def fwd_value(c, state):
        sl = chunk_slice(c)
        q = q_ref[sl, :]
        s = (_qk(q, k_ref[sl, :]) * intra).astype(BF16)
        qf = (q.astype(F32) * q_dec_f).astype(BF16)
        return (jnp.dot(s, v_ref[sl, :], preferred_element_type=F32)
                + jnp.dot(qf, state.astype(BF16), preferred_element_type=F32))

    def bwd_value(c, state):
        qb = (q_ref[chunk_slice(c), :].astype(F32) * q_dec_b).astype(BF16)
        return jnp.dot(qb, state.astype(BF16), preferred_element_type=F32)

    group = 2 if n_chunks % 4 == 0 else 1

    def make_scan(s_ref, kv_ref, chunk_dec, k_dec, chunk_of, value_fn):
        def init(s_init):
            s_ref[...] = jnp.zeros_like(s_ref)
            kv_ref[0] = s_init
            for g in range(1, group):
                kv_ref[g] = kv_term(chunk_of(g - 1), k_dec)

        def step(p0):
            state = s_ref[...]
            vals = []
            for g in range(group):
                state = state * chunk_dec + kv_ref[g]
                vals.append(value_fn(chunk_of(p0 + g), state))
            s_ref[...] = state
            for g in range(group):
                nxt = jnp.minimum(p0 + group - 1 + g, n_chunks - 1)
                kv_ref[g] = kv_term(chunk_of(nxt), k_dec)
            return vals

        return init, step

    fwd_init, fwd_step = make_scan(sf_ref, kvf_ref, chunk_dec_f, k_dec_f, lambda p: p, fwd_value)
    bwd_init, bwd_step = make_scan(sb_ref, kvb_ref, chunk_dec_b, k_dec_b,
                                   lambda p: n_chunks - 1 - p, bwd_value)

    if ctx_len:
        cpos = lax.broadcasted_iota(jnp.int32, (ctx_len, 1), 0).astype(F32)
        kc = kc_ref[...].astype(F32)
        vc = vc_ref[...]
        fwd_init(_kv_outer((kc * jnp.exp(lgf * (ctx_len - 1.0 - cpos))).astype(BF16), vc))
        bwd_init(_kv_outer((kc * jnp.exp(lgb * cpos)).astype(BF16), vc))
    else:
        zero = jnp.zeros(sf_ref.shape, F32)
        fwd_init(zero)
        bwd_init(zero)

    def finish(c, o):
        sl = chunk_slice(c)
        mu = jnp.mean(o, axis=-1, keepdims=True)
        cen = o - mu
        var = jnp.mean(cen * cen, axis=-1, keepdims=True)
        normed = cen * lax.rsqrt(var + EPS)
        o_ref[sl, :] = (normed * gnw_ref[...] * g_ref[sl, :].astype(F32)).astype(BF16)

    half = n_chunks // 2

    def first_half(i, carry):
        p0 = i * group
        for g, (fv, bv) in enumerate(zip(fwd_step(p0), bwd_step(p0))):
            acc_ref[chunk_slice(p0 + g), :] = fv
            acc_ref[chunk_slice(n_chunks - 1 - p0 - g), :] = bv
        return carry

    def second_half(i, carry):
        p0 = half + i * group
        for g, (fv, bv) in enumerate(zip(fwd_step(p0), bwd_step(p0))):
            lo, hi = n_chunks - 1 - p0 - g, p0 + g
            finish(hi, acc_ref[chunk_slice(hi), :] + fv)
            finish(lo, acc_ref[chunk_slice(lo), :] + bv)
        return carry

    lax.fori_loop(0, half // group, first_half, 0, unroll=True)
    lax.fori_loop(0, half // group, second_half, 0, unroll=True)


def _retention(qkv, gate, lg_f, lg_b, gn_w, batch, n_tok, hd, qkv_ctx=None, ctx_len=0):
    n_heads = lg_f.shape[0]
    d = n_heads * hd
    c_len = 2 * RET_CHUNK if n_tok % (8 * RET_CHUNK) == 0 else RET_CHUNK
    n_chunks = n_tok // c_len
    assert n_tok % (2 * c_len) == 0

    def spec(n_rows, part):
        return pl.BlockSpec((n_rows, hd), lambda h, b: (b, part * n_heads + h))

    smem = pl.BlockSpec(memory_space=pltpu.SMEM)
    in_specs = [smem, smem, spec(n_tok, 0), spec(n_tok, 1), spec(n_tok, 2), spec(n_tok, 0),
                pl.BlockSpec((1, hd), lambda h, b: (0, h))]
    args = [lg_f, lg_b, qkv, qkv, qkv, gate, gn_w.reshape(1, d)]
    if ctx_len:
        in_specs += [spec(ctx_len, 1), spec(ctx_len, 2)]
        args += [qkv_ctx, qkv_ctx]
    return pl.pallas_call(
        functools.partial(_retention_kernel, c_len=c_len, n_chunks=n_chunks, ctx_len=ctx_len),
        out_shape=jax.ShapeDtypeStruct((batch * n_tok, d), BF16),
        grid=(n_heads, batch),
        in_specs=in_specs,
        out_specs=pl.BlockSpec((n_tok, hd), lambda h, b: (b, h)),
        scratch_shapes=[pltpu.VMEM((n_tok, hd), F32),
                        pltpu.VMEM((hd, hd), F32),
                        pltpu.VMEM((hd, hd), F32),
                        pltpu.VMEM((2, hd, hd), F32),
                        pltpu.VMEM((2, hd, hd), F32)],
        compiler_params=_params(2),
        name="retention",
    )(*args)


def _rope_tables(seq, hd):
    t = jnp.arange(seq)
    row = (t // GRID_W).astype(F32)
    col = (t % GRID_W).astype(F32)
    axis_dim = hd // 2
    inv_freq = jnp.power(ROPE_BASE, -jnp.arange(0, axis_dim, 2, dtype=F32) / axis_dim)
    ang_r = row[:, None] * inv_freq[None, :]
    ang_c = col[:, None] * inv_freq[None, :]
    cos = jnp.concatenate([jnp.cos(ang_r), jnp.cos(ang_r), jnp.cos(ang_c), jnp.cos(ang_c)], axis=-1)
    sin = jnp.concatenate([-jnp.sin(ang_r), jnp.sin(ang_r), -jnp.sin(ang_c), jnp.sin(ang_c)], axis=-1)
    return cos, sin


def kernel(x, c, ctx, c_ctx, ada_w, ada_b, norm1_w, norm2_w, na_wqkv, na_wo, na_qnorm_w, na_knorm_w, na_rpb, ret_wqkvg, ret_wo, ret_decay_f, ret_decay_b, ret_gn_w, ffn_w13, ffn_w2, moe_router, moe_w13, moe_w2):
    batch, seq, d = x.shape
    ctx_len = ctx.shape[1]
    n_ctx = batch * ctx_len
    depth = ada_w.shape[0]
    na_hd = na_qnorm_w.shape[1]
    ret_heads = ret_decay_f.shape[1]
    ret_hd = d // ret_heads
    assert batch + 1 <= ADA_ROWS

    cond = jnp.zeros((ADA_ROWS, d), F32).at[:batch].set(c).at[batch].set(c_ctx)
    ada = _ada_params(cond, ada_w, ada_b)
    rope = _rope_tables(seq, ret_hd)
    moe_w2_flat = moe_w2.reshape(moe_w2.shape[0], -1, d)

    h = x.reshape(batch * seq, d)
    hc = ctx.reshape(n_ctx, d)
    for i in range(depth):
        j = i // 2
        ctx_out = i < depth - 1
        mods = ada[i].reshape(ADA_ROWS, 6, 1, d)
        sh1, sc1, g1, sh2, sc2, g2 = (mods[:batch, p] for p in range(6))
        csh1, csc1, cg1, csh2, csc2, cg2 = (mods[batch:batch + 1, p] for p in range(6))

        def bf16_layer(w):
            return w[j:j + 1].astype(BF16)

        u = _norm_mod(h, norm1_w[i], sh1, sc1, seq)
        uc = _norm_mod(hc, norm1_w[i], csh1, csc1, n_ctx)
        if i % 2 == 0:
            w_o = na_wo
            qkvc, w_qkv = _mm_na_qkv(uc, na_wqkv, j, na_qnorm_w[j], na_knorm_w[j], cast=True)
            qkv = _mm_na_qkv(u, w_qkv, 0, na_qnorm_w[j], na_knorm_w[j])
            o, oc = _na_attention(qkv, qkvc, na_rpb[j], batch, seq, ctx_len, na_hd)
        else:
            w_o = ret_wo
            lg_f = jax.nn.log_sigmoid(ret_decay_f[j].astype(F32))
            lg_b = jax.nn.log_sigmoid(ret_decay_b[j].astype(F32))
            qkvc, w_qkv = _mm_ret_proj(uc, ret_wqkvg, j, 0, 3 * d, ret_hd, None, n_ctx, False, cast=True)
            gc, w_g = _mm_ret_proj(uc, ret_wqkvg, j, 3 * d, d, ret_hd, None, n_ctx, True, cast=True)
            qkv = _mm_ret_proj(u, w_qkv, 0, 0, 3 * d, ret_hd, rope, seq, False)
            g = _mm_ret_proj(u, w_g, 0, 0, d, ret_hd, None, seq, True)
            o = _retention(qkv, g, lg_f, lg_b, ret_gn_w[j], batch, seq, ret_hd, qkvc, ctx_len)
            if ctx_out:
                oc = _retention(qkvc, gc, lg_f, lg_b, ret_gn_w[j], batch, ctx_len, ret_hd)
        if ctx_out:
            hc, w_o_b = _mm_res(oc, w_o, j, hc, cg1, n_ctx, cast=True)
        else:
            w_o_b = bf16_layer(w_o)
        h = _mm_res(o, w_o_b, 0, h, g1, seq)

        if i % 2 == 0:
            w13, w2, router = ffn_w13, ffn_w2, None
        else:
            w13, w2, router = moe_w13, moe_w2_flat, moe_router[j]
        comb = combc = None
        if ctx_out:
            uc2 = _norm_mod(hc, norm2_w[i], csh2, csc2, n_ctx, router)
            if router is not None:
                uc2, combc = uc2
            actc, w_a, w_b = _mm_swiglu(uc2, w13, w13, j, combc, cast=True)
            hc, w2_b = _mm_res(actc, w2, j, hc, cg2, n_ctx, cast=True)
        else:
            w_a = w_b = bf16_layer(w13)
            w2_b = bf16_layer(w2)
        u2 = _norm_mod(h, norm2_w[i], sh2, sc2, seq, router)
        if router is not None:
            u2, comb = u2
        h = _mm_res(_mm_swiglu(u2, w_a, w_b, 0, comb), w2_b, 0, h, g2, seq)
    return h.reshape(batch, seq, d)
```
